```python
import math
import jax, jax.numpy as jnp
from jax import lax
import numpy as np

D_MODEL = 4096
BATCH = 4
SEQ = 2048
DEPTH = 2
DEC_BATCH = 128
DEC_SEQ = 1
PAST_LEN = 16384
PAGE_SIZE = 128

N_MIXERS = 2
N_RET_LAYERS = (DEPTH + 1) // 2
N_MLSTM_LAYERS = DEPTH // 2
MIX_WIDTH = 2 * D_MODEL
N_HEADS = 12
HEAD_DV = MIX_WIDTH // 16
HEAD_DK = HEAD_DV // 2
MAIN_DK = N_HEADS * HEAD_DK
MAIN_DV = N_HEADS * HEAD_DV
N_MEM = 256
N_MEM_HEADS = 4
MEM_HEAD_DIM = MIX_WIDTH // 16
MEM_DIM = N_MEM_HEADS * MEM_HEAD_DIM
RET_IN = 2 * MAIN_DK + MAIN_DV + MEM_DIM + MIX_WIDTH
MLSTM_IN = RET_IN + MAIN_DV + 2 * N_HEADS
CHUNK = 128
ROPE_THETA = 10000.0
EPS = 1e-6
NEG_INF = -1e30

kernel_name = 'hybrid_retention_mlstm_memory_decode_step'


def rmsnorm(x, g):
    xf = x.astype(jnp.float32)
    xf = xf * lax.rsqrt(jnp.mean(xf * xf, axis=-1, keepdims=True) + EPS)
    return (xf * g.astype(jnp.float32)).astype(x.dtype)


def head_layernorm(h, g):
    mu = jnp.mean(h, axis=-1, keepdims=True)
    var = jnp.mean(jnp.square(h - mu), axis=-1, keepdims=True)
    hn = (h - mu) * lax.rsqrt(var + EPS)
    B, L = h.shape[:2]
    return hn.reshape(B, L, -1) * g.astype(jnp.float32)


def rope(x, pos):
    d = x.shape[-1]
    inv = 1.0 / (ROPE_THETA ** (jnp.arange(0, d, 2, dtype=jnp.float32) / d))
    ang = pos[:, None] * inv[None, :]
    cos = jnp.cos(ang)[None, :, None, :]
    sin = jnp.sin(ang)[None, :, None, :]
    x1, x2 = x[..., : d // 2], x[..., d // 2:]
    return jnp.concatenate([x1 * cos - x2 * sin, x1 * sin + x2 * cos], axis=-1)


def retention_log_decay():
    return jnp.log(1.0 - 2.0 ** (-5.0 - jnp.arange(N_HEADS, dtype=jnp.float32)))


def chunk_len(L):
    return math.gcd(L, CHUNK)


def to_chunks(t, c):
    B, H, L = t.shape[:3]
    t = t.reshape((B, H, L // c, c) + t.shape[3:])
    return jnp.moveaxis(t, 2, 0)


def from_chunks(t):
    t = jnp.moveaxis(t, 0, 2)
    B, H, n, c = t.shape[:4]
    return t.reshape((B, H, n * c) + t.shape[4:])


def retention_scan(q, k, v, s0, log_gamma):
    L = q.shape[2]
    c = chunk_len(L)
    idx = jnp.arange(c, dtype=jnp.float32)
    diff = idx[:, None] - idx[None, :]
    decay_in = jnp.where(diff[None] >= 0.0,
                         jnp.exp(log_gamma[:, None, None] * jnp.maximum(diff, 0.0)[None]), 0.0)
    q_dec = jnp.exp(log_gamma[:, None] * (idx + 1.0)[None, :])
    k_dec = jnp.exp(log_gamma[:, None] * (c - 1.0 - idx)[None, :])
    chunk_dec = jnp.exp(log_gamma * c)

    def step(s, blk):
        qc, kc, vc = blk
        sc = jnp.einsum('bhid,bhjd->bhij', qc, kc) * decay_in[None]
        o = (jnp.einsum('bhij,bhjv->bhiv', sc, vc)
             + jnp.einsum('bhid,bhdv->bhiv', qc * q_dec[None, :, :, None], s))
        s = (s * chunk_dec[None, :, None, None]
             + jnp.einsum('bhjd,bhjv->bhdv', kc * k_dec[None, :, :, None], vc))
        return s, o

    s, o = lax.scan(step, s0, (to_chunks(q, c), to_chunks(k, c), to_chunks(v, c)))
    return from_chunks(o), s


def mlstm_scan(q, k, v, ig, lf, C0, n0, m0):
    L = q.shape[2]
    c = chunk_len(L)
    causal = jnp.tril(jnp.ones((c, c), dtype=bool))

    def step(carry, blk):
        C, n, m = carry
        qc, kc, vc, ic, fc = blk
        b = jnp.cumsum(fc, axis=-1)
        logw = jnp.where(causal, b[..., :, None] - b[..., None, :] + ic[..., None, :], NEG_INF)
        log_prev = b + m[..., None]
        m_t = jnp.maximum(log_prev, jnp.max(logw, axis=-1))
        w = jnp.exp(logw - m_t[..., None])
        a_prev = jnp.exp(log_prev - m_t)
        sc = jnp.einsum('bhid,bhjd->bhij', qc, kc) * w
        num = (jnp.einsum('bhij,bhjv->bhiv', sc, vc)
               + a_prev[..., None] * jnp.einsum('bhid,bhdv->bhiv', qc, C))
        den = jnp.sum(sc, axis=-1) + a_prev * jnp.einsum('bhid,bhd->bhi', qc, n)
        h = num / jnp.maximum(jnp.abs(den), jnp.exp(-m_t))[..., None]
        m_new = m_t[..., -1]
        a_c = jnp.exp(b[..., -1] + m - m_new)
        wk = jnp.exp(b[..., -1:] - b + ic - m_new[..., None])
        C = a_c[..., None, None] * C + jnp.einsum('bhjd,bhjv->bhdv', kc * wk[..., None], vc)
        n = a_c[..., None] * n + jnp.einsum('bhj,bhjd->bhd', wk, kc)
        return (C, n, m_new), h

    xs = (to_chunks(q, c), to_chunks(k, c), to_chunks(v, c), to_chunks(ig, c), to_chunks(lf, c))
    (C, n, m), h = lax.scan(step, (C0, n0, m0), xs)
    return from_chunks(h), C, n, m


def mem_kv(mem, g, w):
    B, M = mem.shape[:2]
    kv = rmsnorm(mem, g) @ w
    k, v = jnp.split(kv, 2, axis=-1)
    return (k.reshape(B, M, N_MEM_HEADS, MEM_HEAD_DIM), v.reshape(B, M, N_MEM_HEADS, MEM_HEAD_DIM))


def mem_attend(qm, mk, mv):
    B, L = qm.shape[:2]
    q = qm.reshape(B, L, N_MEM_HEADS, MEM_HEAD_DIM).astype(jnp.float32) * MEM_HEAD_DIM ** -0.5
    s = jnp.einsum('blhd,bmhd->bhlm', q, mk.astype(jnp.float32))
    p = jax.nn.softmax(s, axis=-1)
    o = jnp.einsum('bhlm,bmhd->blhd', p, mv.astype(jnp.float32))
    return o.reshape(B, L, MEM_DIM)


def retention_branch(h, w_in, s0, pos):
    B, L = h.shape[:2]
    proj = h @ w_in
    c1 = MAIN_DK
    c2 = 2 * MAIN_DK
    c3 = c2 + MAIN_DV
    c4 = c3 + MEM_DIM
    q, k, v, qm, z = jnp.split(proj, [c1, c2, c3, c4], axis=-1)
    q = rope(q.reshape(B, L, N_HEADS, HEAD_DK).astype(jnp.float32), pos)
    k = rope(k.reshape(B, L, N_HEADS, HEAD_DK).astype(jnp.float32), pos) * HEAD_DK ** -0.5
    v = v.reshape(B, L, N_HEADS, HEAD_DV).astype(jnp.float32)
    o, s = retention_scan(q.transpose(0, 2, 1, 3), k.transpose(0, 2, 1, 3), v.transpose(0, 2, 1, 3),
                          s0.astype(jnp.float32), retention_log_decay())
    return o.transpose(0, 2, 1, 3), qm, z, s


def mlstm_branch(h, w_in, b_gate, C0, n0, m0):
    B, L = h.shape[:2]
    proj = h @ w_in
    c1 = MAIN_DK
    c2 = 2 * MAIN_DK
    c3 = c2 + MAIN_DV
    c4 = c3 + MEM_DIM
    c5 = c4 + MIX_WIDTH
    c6 = c5 + MAIN_DV
    q, k, v, qm, z, og, gates = jnp.split(proj, [c1, c2, c3, c4, c5, c6], axis=-1)
    gates = gates.astype(jnp.float32) + b_gate.astype(jnp.float32)
    ig = gates[..., :N_HEADS].transpose(0, 2, 1)
    lf = jax.nn.log_sigmoid(gates[..., N_HEADS:]).transpose(0, 2, 1)
    q = q.reshape(B, L, N_HEADS, HEAD_DK).astype(jnp.float32).transpose(0, 2, 1, 3)
    k = (k.reshape(B, L, N_HEADS, HEAD_DK).astype(jnp.float32) * HEAD_DK ** -0.5).transpose(0, 2, 1, 3)
    v = v.reshape(B, L, N_HEADS, HEAD_DV).astype(jnp.float32).transpose(0, 2, 1, 3)
    hc, C, n, m = mlstm_scan(q, k, v, ig, lf, C0.astype(jnp.float32), n0.astype(jnp.float32),
                             m0.astype(jnp.float32))
    hc = hc.transpose(0, 2, 1, 3) * jax.nn.sigmoid(og.astype(jnp.float32)).reshape(B, L, N_HEADS, HEAD_DV)
    return hc, qm, z, C, n, m


def setup_inputs(seed: int = 0) -> dict:
    key = jax.random.key(seed)
    ks = jax.random.split(key, 20)

    def nrm(k, shape, s):
        return jax.random.normal(k, shape, jnp.float32) * s

    b_i = nrm(ks[13], (N_MLSTM_LAYERS, N_HEADS), 0.1)
    b_f = jnp.linspace(3.0, 6.0, N_HEADS, dtype=jnp.float32)[None, :] + nrm(ks[14], (N_MLSTM_LAYERS, N_HEADS), 0.1)
    return {
        'x_prompt': nrm(ks[0], (BATCH, SEQ, D_MODEL), 1.0),
        'x_sample': nrm(ks[1], (DEC_BATCH, DEC_SEQ, D_MODEL), 1.0),
        'mem_prompt': nrm(ks[2], (BATCH, N_MEM, D_MODEL), 1.0),
        'state_ret': nrm(ks[3], (N_RET_LAYERS, DEC_BATCH, N_HEADS, HEAD_DK, HEAD_DV), 0.1),
        'state_mlstm_C': nrm(ks[4], (N_MLSTM_LAYERS, DEC_BATCH, N_HEADS, HEAD_DK, HEAD_DV), 0.1),
        'state_mlstm_n': nrm(ks[5], (N_MLSTM_LAYERS, DEC_BATCH, N_HEADS, HEAD_DK), 0.1),
        'state_mlstm_m': nrm(ks[6], (N_MLSTM_LAYERS, DEC_BATCH, N_HEADS), 1.0),
        'cache_mem_k': nrm(ks[7], (DEPTH, DEC_BATCH, N_MEM, N_MEM_HEADS, MEM_HEAD_DIM), 1.0),
        'cache_mem_v': nrm(ks[8], (DEPTH, DEC_BATCH, N_MEM, N_MEM_HEADS, MEM_HEAD_DIM), 1.0),
        'g_pre': 1.0 + nrm(ks[9], (DEPTH, D_MODEL), 0.02),
        'g_post': 1.0 + nrm(ks[10], (DEPTH, D_MODEL), 0.02),
        'w_in_ret': nrm(ks[11], (N_RET_LAYERS, D_MODEL, RET_IN), D_MODEL ** -0.5),
        'w_in_mlstm': nrm(ks[12], (N_MLSTM_LAYERS, D_MODEL, MLSTM_IN), D_MODEL ** -0.5),
        'b_gate_mlstm': jnp.concatenate([b_i, b_f], axis=-1),
        'g_head': 1.0 + nrm(ks[15], (DEPTH, MAIN_DV), 0.02),
        'w_out': nrm(ks[16], (DEPTH, MIX_WIDTH, D_MODEL), MIX_WIDTH ** -0.5),
        'g_mem': 1.0 + nrm(ks[17], (DEPTH, D_MODEL), 0.02),
        'w_mem_kv': nrm(ks[18], (DEPTH, D_MODEL, 2 * MEM_DIM), D_MODEL ** -0.5),
    }


def reference(x_prompt, x_sample, mem_prompt, state_ret, state_mlstm_C, state_mlstm_n, state_mlstm_m,
              cache_mem_k, cache_mem_v, g_pre, g_post, w_in_ret, w_in_mlstm, b_gate_mlstm, g_head,
              w_out, g_mem, w_mem_kv):

    def trunk(x, pos, mem_ks, mem_vs, s_ret, s_C, s_n, s_m):
        new_ret, new_C, new_n, new_m = [], [], [], []
        for l in range(DEPTH):
            j = l // N_MIXERS
            h = rmsnorm(x, g_pre[l])
            if l % N_MIXERS == 0:
                main, qm, z, s = retention_branch(h, w_in_ret[j], s_ret[j], pos)
                new_ret.append(s)
            else:
                main, qm, z, C, n, m = mlstm_branch(h, w_in_mlstm[j], b_gate_mlstm[j], s_C[j], s_n[j], s_m[j])
                new_C.append(C)
                new_n.append(n)
                new_m.append(m)
            u = jnp.concatenate([head_layernorm(main, g_head[l]), mem_attend(qm, mem_ks[l], mem_vs[l])], axis=-1)
            u = u * jax.nn.silu(z.astype(jnp.float32))
            out = u.astype(x.dtype) @ w_out[l]
            x = x + rmsnorm(out, g_post[l])
        return x, jnp.stack(new_ret), jnp.stack(new_C), jnp.stack(new_n), jnp.stack(new_m)

    Bp = x_prompt.shape[0]
    kv_p = [mem_kv(mem_prompt, g_mem[l], w_mem_kv[l]) for l in range(DEPTH)]
    mem_k_prompt = jnp.stack([kv[0] for kv in kv_p])
    mem_v_prompt = jnp.stack([kv[1] for kv in kv_p])
    pos_p = jnp.arange(x_prompt.shape[1], dtype=jnp.float32)
    ret0 = jnp.zeros((N_RET_LAYERS, Bp, N_HEADS, HEAD_DK, HEAD_DV), jnp.float32)
    C0 = jnp.zeros((N_MLSTM_LAYERS, Bp, N_HEADS, HEAD_DK, HEAD_DV), jnp.float32)
    n0 = jnp.zeros((N_MLSTM_LAYERS, Bp, N_HEADS, HEAD_DK), jnp.float32)
    m0 = jnp.full((N_MLSTM_LAYERS, Bp, N_HEADS), NEG_INF, jnp.float32)
    y_prompt, ret_prompt, C_prompt, n_prompt, m_prompt = trunk(
        x_prompt, pos_p, mem_k_prompt, mem_v_prompt, ret0, C0, n0, m0)

    pos_s = PAST_LEN + jnp.arange(x_sample.shape[1], dtype=jnp.float32)
    y_sample, ret_sample, C_sample, n_sample, m_sample = trunk(
        x_sample, pos_s, cache_mem_k, cache_mem_v, state_ret, state_mlstm_C, state_mlstm_n, state_mlstm_m)

    return (y_prompt, y_sample, ret_prompt, C_prompt, n_prompt, m_prompt, mem_k_prompt, mem_v_prompt,
            ret_sample, C_sample, n_sample, m_sample)
```

```python
import functools
import math

import jax
import jax.numpy as jnp
from jax import lax
from jax.experimental import pallas as pl
from jax.experimental.pallas import tpu as pltpu

F32 = jnp.float32
BF16 = jnp.bfloat16

D_MODEL = 4096
DEPTH = 2
PAST_LEN = 16384
MIX_WIDTH = 2 * D_MODEL
N_HEADS = 12
HEAD_DV = MIX_WIDTH // 16
HEAD_DK = HEAD_DV // 2
MAIN_DK = N_HEADS * HEAD_DK
MAIN_DV = N_HEADS * HEAD_DV
N_MEM = 256
N_MEM_HEADS = 4
MEM_HEAD_DIM = MIX_WIDTH // 16
MEM_DIM = N_MEM_HEADS * MEM_HEAD_DIM
CHUNK = 128
ROPE_THETA = 10000.0
EPS = 1e-6
NEG_INF = -1e30

VMEM_LIMIT_BYTES = 56 * 1024 * 1024
MM_TILE_M = 1024
MM_TILE_N = 512
SCAN_TILE_L = 512
NORM_TILE_M = 256

_NT = (((1,), (1,)), ((), ()))
_TN = (((0,), (0,)), ((), ()))


def _cparams(sem):
    return pltpu.CompilerParams(dimension_semantics=sem, vmem_limit_bytes=VMEM_LIMIT_BYTES)


def _sigmoid(x):
    return 1.0 / (1.0 + jnp.exp(-x))


def _log_sigmoid(x):
    return jnp.minimum(x, 0.0) - jnp.log(1.0 + jnp.exp(-jnp.abs(x)))


def _rope(x, cos, sin):
    half = HEAD_DK // 2
    x1, x2 = x[:, :half], x[:, half:]
    return jnp.concatenate([x1 * cos - x2 * sin, x1 * sin + x2 * cos], axis=-1)


def _head_norm_gate(main, g, z):
    mu = jnp.mean(main, axis=-1, keepdims=True)
    d = main - mu
    var = jnp.mean(d * d, axis=-1, keepdims=True)
    hn = d * lax.rsqrt(var + EPS)
    return hn * g * (z * _sigmoid(z))


def _dot(a, b):
    return jnp.dot(a, b, preferred_element_type=F32)


def _rmsnorm_body(x_ref, g_ref, o_ref):
    x = x_ref[...]
    ms = jnp.mean(x * x, axis=-1, keepdims=True)
    o_ref[...] = (x * lax.rsqrt(ms + EPS) * g_ref[...]).astype(o_ref.dtype)


def _rmsnorm(x, g, name):
    m, d = x.shape
    tm = min(NORM_TILE_M, m)
    return pl.pallas_call(
        _rmsnorm_body,
        grid=(m // tm,),
        in_specs=[pl.BlockSpec((tm, d), lambda i: (i, 0)),
                  pl.BlockSpec((1, d), lambda i: (0, 0))],
        out_specs=pl.BlockSpec((tm, d), lambda i: (i, 0)),
        out_shape=jax.ShapeDtypeStruct((m, d), BF16),
        compiler_params=_cparams(("parallel",)),
        name=name,
    )(x, g.reshape(1, d))


def _mm_body(a_ref, w_ref, o_ref):
    o_ref[...] = _dot(a_ref[...], w_ref[...]).astype(o_ref.dtype)


def _matmul(a, w, out_dtype, name):
    m, k = a.shape
    n = w.shape[1]
    tm = min(MM_TILE_M, m)
    tn = min(MM_TILE_N, n)
    return pl.pallas_call(
        _mm_body,
        grid=(m // tm, n // tn),
        in_specs=[pl.BlockSpec((tm, k), lambda i, j: (i, 0)),
                  pl.BlockSpec((k, tn), lambda i, j: (0, j))],
        out_specs=pl.BlockSpec((tm, tn), lambda i, j: (i, j)),
        out_shape=jax.ShapeDtypeStruct((m, n), out_dtype),
        compiler_params=_cparams(("parallel", "parallel")),
        name=name,
    )(a, w)


def _outproj_body(um_ref, ue_ref, w_ref, o_ref, acc_ref):
    k = pl.program_id(2)
    last = pl.num_programs(2) - 1

    @pl.when(k == 0)
    def _():
        acc_ref[...] = jnp.zeros_like(acc_ref)

    @pl.when(k < last)
    def _():
        acc_ref[...] += _dot(um_ref[...], w_ref[...])

    @pl.when(k == last)
    def _():
        o_ref[...] = acc_ref[...] + _dot(ue_ref[...], w_ref[...])


def _outproj(u_main, u_mem, w, name):
    m = u_main.shape[0]
    n = w.shape[1]
    tk = MEM_DIM
    n_main = MAIN_DV // tk
    tm = min(MM_TILE_M, m)
    tn = min(MM_TILE_N, n)
    return pl.pallas_call(
        _outproj_body,
        grid=(m // tm, n // tn, n_main + 1),
        in_specs=[pl.BlockSpec((tm, tk), lambda i, j, k: (i, jnp.minimum(k, n_main - 1))),
                  pl.BlockSpec((tm, tk), lambda i, j, k: (i, 0)),
                  pl.BlockSpec((tk, tn), lambda i, j, k: (k, j))],
        out_specs=pl.BlockSpec((tm, tn), lambda i, j, k: (i, j)),
        out_shape=jax.ShapeDtypeStruct((m, n), F32),
        scratch_shapes=[pltpu.VMEM((tm, tn), F32)],
        compiler_params=_cparams(("parallel", "parallel", "arbitrary")),
        name=name,
    )(u_main, u_mem, w)


def _post_body(o_ref, x_ref, gp_ref, gn_ref, y_ref, h_ref):
    o = o_ref[...]
    ms = jnp.mean(o * o, axis=-1, keepdims=True)
    y = x_ref[...] + o * lax.rsqrt(ms + EPS) * gp_ref[...]
    y_ref[...] = y
    ms2 = jnp.mean(y * y, axis=-1, keepdims=True)
    h_ref[...] = (y * lax.rsqrt(ms2 + EPS) * gn_ref[...]).astype(h_ref.dtype)


def _post_last_body(o_ref, x_ref, gp_ref, y_ref):
    o = o_ref[...]
    ms = jnp.mean(o * o, axis=-1, keepdims=True)
    y_ref[...] = x_ref[...] + o * lax.rsqrt(ms + EPS) * gp_ref[...]


def _post(out, x, g_post, g_next, name):
    m, d = x.shape
    tm = min(NORM_TILE_M, m)
    row = pl.BlockSpec((tm, d), lambda i: (i, 0))
    vec = pl.BlockSpec((1, d), lambda i: (0, 0))
    if g_next is None:
        return pl.pallas_call(
            _post_last_body, grid=(m // tm,),
            in_specs=[row, row, vec], out_specs=row,
            out_shape=jax.ShapeDtypeStruct((m, d), F32),
            compiler_params=_cparams(("parallel",)), name=name,
        )(out, x, g_post.reshape(1, d)), None
    return pl.pallas_call(
        _post_body, grid=(m // tm,),
        in_specs=[row, row, vec, vec], out_specs=[row, row],
        out_shape=[jax.ShapeDtypeStruct((m, d), F32), jax.ShapeDtypeStruct((m, d), BF16)],
        compiler_params=_cparams(("parallel",)), name=name,
    )(out, x, g_post.reshape(1, d), g_next.reshape(1, d))


def _ret_prompt_body(q_ref, k_ref, v_ref, z_ref, cos_ref, sin_ref, din_ref, qd_ref, kd_ref,
                     cd_ref, g_ref, u_ref, s_ref):
    @pl.when(pl.program_id(2) == 0)
    def _():
        s_ref[...] = jnp.zeros_like(s_ref)

    din = din_ref[0]
    qd = qd_ref[0]
    kd = kd_ref[0]
    cd = cd_ref[0][:, :1]
    g = g_ref[...]
    for ci in range(SCAN_TILE_L // CHUNK):
        sl = slice(ci * CHUNK, (ci + 1) * CHUNK)
        cos, sin = cos_ref[sl, :], sin_ref[sl, :]
        q = _rope(q_ref[0, sl, :], cos, sin)
        k = _rope(k_ref[0, sl, :], cos, sin) * HEAD_DK ** -0.5
        v = v_ref[0, sl, :]
        s = s_ref[0, 0]
        sc = lax.dot_general(q.astype(BF16), k.astype(BF16), _NT, preferred_element_type=F32) * din
        o = _dot(sc.astype(BF16), v) + _dot((q * qd).astype(BF16), s.astype(BF16))
        s_ref[0, 0] = s * cd + lax.dot_general((k * kd).astype(BF16), v, _TN,
                                               preferred_element_type=F32)
        u_ref[0, sl, :] = _head_norm_gate(o, g, z_ref[0, sl, :]).astype(u_ref.dtype)


def _ret_prompt(qk, v, z, cos, sin, g_head):
    b, l, _ = qk.shape
    tl, c = SCAN_TILE_L, CHUNK
    lg = jnp.log(1.0 - 2.0 ** (-5.0 - jnp.arange(N_HEADS, dtype=F32)))
    idx = jnp.arange(c, dtype=F32)
    diff = idx[:, None] - idx[None, :]
    decay_in = jnp.where(diff[None] >= 0.0,
                         jnp.exp(lg[:, None, None] * jnp.maximum(diff, 0.0)[None]), 0.0)
    q_dec = jnp.exp(lg[:, None] * (idx + 1.0)[None, :])[:, :, None]
    k_dec = jnp.exp(lg[:, None] * (c - 1.0 - idx)[None, :])[:, :, None]
    chunk_dec = jnp.broadcast_to(jnp.exp(lg * c)[:, None, None], (N_HEADS, 1, 128))
    return pl.pallas_call(
        _ret_prompt_body,
        grid=(b, N_HEADS, l // tl),
        in_specs=[
            pl.BlockSpec((1, tl, HEAD_DK), lambda i, h, t: (i, t, h)),
            pl.BlockSpec((1, tl, HEAD_DK), lambda i, h, t: (i, t, N_HEADS + h)),
            pl.BlockSpec((1, tl, HEAD_DV), lambda i, h, t: (i, t, h)),
            pl.BlockSpec((1, tl, HEAD_DV), lambda i, h, t: (i, t, h)),
            pl.BlockSpec((tl, HEAD_DK // 2), lambda i, h, t: (t, 0)),
            pl.BlockSpec((tl, HEAD_DK // 2), lambda i, h, t: (t, 0)),
            pl.BlockSpec((1, c, c), lambda i, h, t: (h, 0, 0)),
            pl.BlockSpec((1, c, 1), lambda i, h, t: (h, 0, 0)),
            pl.BlockSpec((1, c, 1), lambda i, h, t: (h, 0, 0)),
            pl.BlockSpec((1, 1, 128), lambda i, h, t: (h, 0, 0)),
            pl.BlockSpec((1, HEAD_DV), lambda i, h, t: (0, h)),
        ],
        out_specs=[
            pl.BlockSpec((1, tl, HEAD_DV), lambda i, h, t: (i, t, h)),
            pl.BlockSpec((1, 1, HEAD_DK, HEAD_DV), lambda i, h, t: (i, h, 0, 0)),
        ],
        out_shape=[jax.ShapeDtypeStruct((b, l, MAIN_DV), BF16),
                   jax.ShapeDtypeStruct((b, N_HEADS, HEAD_DK, HEAD_DV), F32)],
        compiler_params=_cparams(("parallel", "parallel", "arbitrary")),
        name="ret_prompt",
    )(qk, qk, v, z, cos, sin, decay_in, q_dec, k_dec, chunk_dec, g_head.reshape(1, MAIN_DV))


def _col_from_row(row, eye):
    return jnp.sum(jnp.where(eye, row, 0.0), axis=1, keepdims=True)


def _mlstm_prompt_body(q_ref, k_ref, v_ref, og_ref, z_ref, gi_ref, gf_ref, bi_ref, bf_ref, g_ref,
                       u_ref, c_ref, n_ref, m_ref):
    @pl.when(pl.program_id(2) == 0)
    def _():
        c_ref[...] = jnp.zeros_like(c_ref)
        n_ref[...] = jnp.zeros_like(n_ref)
        m_ref[...] = jnp.full_like(m_ref, NEG_INF)

    c = CHUNK
    g = g_ref[...]
    b_i = bi_ref[0]
    b_f = bf_ref[0]
    ri = lax.broadcasted_iota(jnp.int32, (c, c), 0)
    cj = lax.broadcasted_iota(jnp.int32, (c, c), 1)
    eye = ri == cj
    causal = ri >= cj
    lane8 = lax.broadcasted_iota(jnp.int32, (8, c), 1)
    for ci in range(SCAN_TILE_L // CHUNK):
        sl = slice(ci * CHUNK, (ci + 1) * CHUNK)
        ic = gi_ref[0, 0, :, sl] + b_i
        lf = _log_sigmoid(gf_ref[0, 0, :, sl] + b_f)
        cs = jnp.broadcast_to(lf, (8, c))
        sh = 1
        while sh < c:
            cs = cs + jnp.where(lane8 >= sh, pltpu.roll(cs, sh, 1), 0.0)
            sh *= 2
        b_row = cs[:1, :]
        r_row = ic - b_row
        b_col = _col_from_row(b_row, eye)
        r_col = _col_from_row(r_row, eye)
        b_last = b_row[:, c - 1:]
        m_prev = m_ref[0, 0][:, :1]

        logw = jnp.where(causal, b_col + r_row, NEG_INF)
        log_prev = b_col + m_prev
        m_t = jnp.maximum(log_prev, jnp.max(logw, axis=1, keepdims=True))
        w = jnp.exp(logw - m_t)
        a_prev = jnp.exp(log_prev - m_t)

        q = q_ref[0, sl, :]
        k = k_ref[0, sl, :] * HEAD_DK ** -0.5
        v = v_ref[0, sl, :]
        qb, kb = q.astype(BF16), k.astype(BF16)
        cm = c_ref[0, 0]
        nv = n_ref[0, 0]
        sc = lax.dot_general(qb, kb, _NT, preferred_element_type=F32) * w
        num = _dot(sc.astype(BF16), v) + a_prev * _dot(qb, cm.astype(BF16))
        qn = jnp.sum(qb.astype(F32) * nv.astype(BF16).astype(F32), axis=1, keepdims=True)
        den = jnp.sum(sc, axis=1, keepdims=True) + a_prev * qn
        hc = num / jnp.maximum(jnp.abs(den), jnp.exp(-m_t))

        m_new = m_t[c - 1:, :]
        a_c = jnp.exp(b_last + m_prev - m_new)
        wk_col = jnp.exp(b_last + r_col - m_new)
        wk_row = jnp.exp(b_last + r_row - m_new)
        c_ref[0, 0] = a_c * cm + lax.dot_general((k * wk_col).astype(BF16), v, _TN,
                                                 preferred_element_type=F32)
        wk8 = jnp.broadcast_to(wk_row, (8, c)).astype(BF16)
        n_ref[0, 0] = a_c * nv + _dot(wk8, kb)[:1, :]
        m_ref[0, 0] = jnp.broadcast_to(m_new, (1, 128))

        main = hc * _sigmoid(og_ref[0, sl, :])
        u_ref[0, sl, :] = _head_norm_gate(main, g, z_ref[0, sl, :]).astype(u_ref.dtype)


def _mlstm_prompt(qk, v, og, z, gates_t, bias, g_head):
    b, l, _ = qk.shape
    tl = SCAN_TILE_L
    bias3 = bias.reshape(2 * N_HEADS, 1, 1)
    return pl.pallas_call(
        _mlstm_prompt_body,
        grid=(b, N_HEADS, l // tl),
        in_specs=[
            pl.BlockSpec((1, tl, HEAD_DK), lambda i, h, t: (i, t, h)),
            pl.BlockSpec((1, tl, HEAD_DK), lambda i, h, t: (i, t, N_HEADS + h)),
            pl.BlockSpec((1, tl, HEAD_DV), lambda i, h, t: (i, t, h)),
            pl.BlockSpec((1, tl, HEAD_DV), lambda i, h, t: (i, t, h)),
            pl.BlockSpec((1, tl, HEAD_DV), lambda i, h, t: (i, t, h)),
            pl.BlockSpec((1, 1, 1, tl), lambda i, h, t: (i, h, 0, t)),
            pl.BlockSpec((1, 1, 1, tl), lambda i, h, t: (i, N_HEADS + h, 0, t)),
            pl.BlockSpec((1, 1, 1), lambda i, h, t: (h, 0, 0)),
            pl.BlockSpec((1, 1, 1), lambda i, h, t: (N_HEADS + h, 0, 0)),
            pl.BlockSpec((1, HEAD_DV), lambda i, h, t: (0, h)),
        ],
        out_specs=[
            pl.BlockSpec((1, tl, HEAD_DV), lambda i, h, t: (i, t, h)),
            pl.BlockSpec((1, 1, HEAD_DK, HEAD_DV), lambda i, h, t: (i, h, 0, 0)),
            pl.BlockSpec((1, 1, 1, HEAD_DK), lambda i, h, t: (i, h, 0, 0)),
            pl.BlockSpec((1, 1, 1, 128), lambda i, h, t: (i, h, 0, 0)),
        ],
        out_shape=[jax.ShapeDtypeStruct((b, l, MAIN_DV), BF16),
                   jax.ShapeDtypeStruct((b, N_HEADS, HEAD_DK, HEAD_DV), F32),
                   jax.ShapeDtypeStruct((b, N_HEADS, 1, HEAD_DK), F32),
                   jax.ShapeDtypeStruct((b, N_HEADS, 1, 128), F32)],
        compiler_params=_cparams(("parallel", "parallel", "arbitrary")),
        name="mlstm_prompt",
    )(qk, qk, v, og, z, gates_t, gates_t, bias3, bias3, g_head.reshape(1, MAIN_DV))


def _memattn_prompt_body(q_ref, k_ref, v_ref, z_ref, u_ref):
    q = (q_ref[0] * MEM_HEAD_DIM ** -0.5).astype(BF16)
    s = lax.dot_general(q, k_ref[0].astype(BF16), _NT, preferred_element_type=F32)
    p = jnp.exp(s - jnp.max(s, axis=-1, keepdims=True))
    p = p / jnp.sum(p, axis=-1, keepdims=True)
    o = _dot(p.astype(BF16), v_ref[0].astype(BF16))
    z = z_ref[0]
    u_ref[0] = (o * (z * _sigmoid(z))).astype(u_ref.dtype)


def _memattn_prompt(qm, mk, mv, z):
    b, l, _ = qm.shape
    tl = SCAN_TILE_L
    zoff = MAIN_DV // MEM_HEAD_DIM
    return pl.pallas_call(
        _memattn_prompt_body,
        grid=(b, N_MEM_HEADS, l // tl),
        in_specs=[
            pl.BlockSpec((1, tl, MEM_HEAD_DIM), lambda i, h, t: (i, t, h)),
            pl.BlockSpec((1, N_MEM, MEM_HEAD_DIM), lambda i, h, t: (i, 0, h)),
            pl.BlockSpec((1, N_MEM, MEM_HEAD_DIM), lambda i, h, t: (i, 0, h)),
            pl.BlockSpec((1, tl, MEM_HEAD_DIM), lambda i, h, t: (i, t, zoff + h)),
        ],
        out_specs=pl.BlockSpec((1, tl, MEM_HEAD_DIM), lambda i, h, t: (i, t, h)),
        out_shape=jax.ShapeDtypeStruct((b, l, MEM_DIM), BF16),
        compiler_params=_cparams(("parallel", "parallel", "parallel")),
        name="memattn_prompt",
    )(qm, mk, mv, z)


def _col_bcast(row16, width):
    ones = jnp.ones((16, width), BF16)
    return lax.dot_general(row16, ones, _TN, preferred_element_type=F32)


def _first_row16(x):
    n = x.shape[1]
    r = lax.broadcasted_iota(jnp.int32, (16, n), 0)
    return jnp.where(r == 0, jnp.broadcast_to(x, (16, n)), 0.0).astype(BF16)


def _decode_body(a_ref, w_ref, e_ref, q_ref, k_ref, v_ref, z_ref, g_ref, s_ref, *rest,
                 use_rope, is_mlstm):
    if use_rope:
        cos_ref, sin_ref = rest[:2]
        rest = rest[2:]
    if is_mlstm:
        og_ref, n_ref, u_ref, so_ref, no_ref = rest
    else:
        u_ref, so_ref = rest
    idx = pl.program_id(0) * N_HEADS + pl.program_id(1)
    a = a_ref[idx]
    w = w_ref[idx]

    q = q_ref[0]
    k = k_ref[0]
    if use_rope:
        q = _rope(q, cos_ref[...], sin_ref[...])
        k = _rope(k, cos_ref[...], sin_ref[...])
    k = k * HEAD_DK ** -0.5
    v = v_ref[0]
    vf = v.astype(F32)
    qb = q.astype(BF16)
    kb = k.astype(BF16)
    s = s_ref[0, 0]

    qk = jnp.sum(qb.astype(F32) * kb.astype(F32), axis=1, keepdims=True)
    sc = qk * w
    q_cols = _col_bcast(_first_row16(q), HEAD_DV)
    qs = jnp.sum(q_cols * s, axis=0, keepdims=True)
    num = sc.astype(BF16).astype(F32) * vf + a * qs
    kv = lax.dot_general(_first_row16(k * w), jnp.broadcast_to(v, (16, HEAD_DV)), _TN,
                         preferred_element_type=F32)
    so_ref[0, 0] = a * s + kv

    if is_mlstm:
        nv = n_ref[0, 0]
        qn = jnp.sum(qb.astype(F32) * nv.astype(BF16).astype(F32), axis=1, keepdims=True)
        den = sc + a * qn
        main = num / jnp.maximum(jnp.abs(den), e_ref[idx])
        wb = (jnp.zeros((1, 1), F32) + w).astype(BF16).astype(F32)
        no_ref[0, 0] = a * nv + wb * kb.astype(F32)
        main = main * _sigmoid(og_ref[0])
    else:
        main = num
    u_ref[0] = _head_norm_gate(main, g_ref[...], z_ref[0]).astype(u_ref.dtype)


def _decode(a, w, e, qk, v, z, g_head, state, rope_tabs=None, og=None, n_state=None):
    b = qk.shape[0]
    is_mlstm = og is not None
    use_rope = rope_tabs is not None
    dk_spec = lambda off: pl.BlockSpec((1, 1, HEAD_DK), lambda i, h, *_: (i, 0, off + h))
    dv_spec = pl.BlockSpec((1, 1, HEAD_DV), lambda i, h, *_: (i, 0, h))
    st_spec = pl.BlockSpec((1, 1, HEAD_DK, HEAD_DV), lambda i, h, *_: (i, h, 0, 0))
    n_spec = pl.BlockSpec((1, 1, 1, HEAD_DK), lambda i, h, *_: (i, h, 0, 0))
    in_specs = [dk_spec(0), dk_spec(N_HEADS), dv_spec, dv_spec,
                pl.BlockSpec((1, HEAD_DV), lambda i, h, *_: (0, h)), st_spec]
    args = [qk, qk, v, z, g_head.reshape(1, MAIN_DV), state]
    out_specs = [dv_spec, st_spec]
    out_shape = [jax.ShapeDtypeStruct((b, 1, MAIN_DV), BF16),
                 jax.ShapeDtypeStruct(state.shape, F32)]
    if use_rope:
        tab = pl.BlockSpec((1, HEAD_DK // 2), lambda i, h, *_: (0, 0))
        in_specs += [tab, tab]
        args += list(rope_tabs)
    if is_mlstm:
        in_specs += [dv_spec, n_spec]
        args += [og, n_state]
        out_specs.append(n_spec)
        out_shape.append(jax.ShapeDtypeStruct(n_state.shape, F32))
    return pl.pallas_call(
        functools.partial(_decode_body, use_rope=use_rope, is_mlstm=is_mlstm),
        grid_spec=pltpu.PrefetchScalarGridSpec(
            num_scalar_prefetch=3, grid=(b, N_HEADS),
            in_specs=in_specs, out_specs=out_specs),
        out_shape=out_shape,
        compiler_params=_cparams(("parallel", "parallel")),
        name="mlstm_decode" if is_mlstm else "ret_decode",
    )(a, w, e, *args)


def _gate_decode_body(gi_ref, gf_ref, bi_ref, bf_ref, m_ref, mo_ref, a_ref, w_ref, e_ref):
    ic = gi_ref[...] + bi_ref[...]
    lf = _log_sigmoid(gf_ref[...] + bf_ref[...])
    log_prev = lf + m_ref[...]
    m_t = jnp.maximum(log_prev, ic)
    mo_ref[...] = m_t
    a_ref[...] = jnp.exp(log_prev - m_t)
    w_ref[...] = jnp.exp(ic - m_t)
    e_ref[...] = jnp.exp(-m_t)


def _gate_decode(g_i, g_f, b_i, b_f, m):
    sds = jax.ShapeDtypeStruct(m.shape, F32)
    return pl.pallas_call(
        _gate_decode_body, out_shape=[sds, sds, sds, sds], name="mlstm_decode_gates",
    )(g_i, g_f, b_i.reshape(1, N_HEADS), b_f.reshape(1, N_HEADS), m)


def _memattn_decode_body(q_ref, k_ref, v_ref, z_ref, u_ref):
    for hh in range(N_MEM_HEADS):
        sl = slice(hh * MEM_HEAD_DIM, (hh + 1) * MEM_HEAD_DIM)
        q = (q_ref[0][:, sl] * MEM_HEAD_DIM ** -0.5).astype(BF16).astype(F32)
        s = jnp.sum(k_ref[0, :, sl] * q, axis=1, keepdims=True)
        p = jnp.exp(s - jnp.max(s, axis=0, keepdims=True))
        p = p / jnp.sum(p, axis=0, keepdims=True)
        o = jnp.sum(p * v_ref[0, :, sl], axis=0, keepdims=True)
        z = z_ref[0][:, sl]
        u_ref[0, :, sl] = (o * (z * _sigmoid(z))).astype(u_ref.dtype)


def _memattn_decode(qm, mk, mv, z):
    b = qm.shape[0]
    row = pl.BlockSpec((1, 1, MEM_DIM), lambda i: (i, 0, 0))
    kv = pl.BlockSpec((1, N_MEM, MEM_DIM), lambda i: (i, 0, 0))
    return pl.pallas_call(
        _memattn_decode_body,
        grid=(b,),
        in_specs=[row, kv, kv, pl.BlockSpec((1, 1, MEM_DIM), lambda i: (i, 0, MAIN_DV // MEM_DIM))],
        out_specs=row,
        out_shape=jax.ShapeDtypeStruct((b, 1, MEM_DIM), BF16),
        compiler_params=_cparams(("parallel",)),
        name="memattn_decode",
    )(qm, mk, mv, z)


def _rope_tables(pos):
    d = HEAD_DK
    inv = 1.0 / (ROPE_THETA ** (jnp.arange(0, d, 2, dtype=F32) / d))
    ang = pos[:, None] * inv[None, :]
    return jnp.cos(ang), jnp.sin(ang)


def _split_weights(w, with_mlstm):
    c1 = 2 * MAIN_DK
    c2 = c1 + MAIN_DV
    c3 = c2 + MEM_DIM
    c4 = c3 + MIX_WIDTH
    parts = {"qk": w[:, :c1], "v": w[:, c1:c2], "qm": w[:, c2:c3], "z": w[:, c3:c4]}
    if with_mlstm:
        c5 = c4 + MAIN_DV
        parts["og"] = w[:, c4:c5]
        parts["gates"] = jnp.pad(w[:, c5:], ((0, 0), (0, 128 - 2 * N_HEADS)))
    return {k: p.astype(BF16) for k, p in parts.items()}


_PROJ_DTYPES = {"qk": F32, "v": BF16, "qm": F32, "z": F32, "og": F32, "gates": F32}


def _in_proj(h, wparts, tag):
    return {k: _matmul(h, w, _PROJ_DTYPES[k], f"inproj_{k}_{tag}") for k, w in wparts.items()}


def kernel(x_prompt, x_sample, mem_prompt, state_ret, state_mlstm_C, state_mlstm_n, state_mlstm_m,
           cache_mem_k, cache_mem_v, g_pre, g_post, w_in_ret, w_in_mlstm, b_gate_mlstm, g_head,
           w_out, g_mem, w_mem_kv):
    bp, lp, _ = x_prompt.shape
    bs = x_sample.shape[0]
    mp = bp * lp

    w_in = [_split_weights(w_in_ret[0], False), _split_weights(w_in_mlstm[0], True)]
    w_o = [w_out[l].astype(BF16) for l in range(DEPTH)]

    mem2d = mem_prompt.reshape(bp * N_MEM, D_MODEL)
    mem_k, mem_v = [], []
    for l in range(DEPTH):
        hm = _rmsnorm(mem2d, g_mem[l], f"mem_norm_{l}")
        wkv = w_mem_kv[l].astype(BF16)
        mem_k.append(_matmul(hm, wkv[:, :MEM_DIM], F32, f"mem_k_{l}").reshape(bp, N_MEM, MEM_DIM))
        mem_v.append(_matmul(hm, wkv[:, MEM_DIM:], F32, f"mem_v_{l}").reshape(bp, N_MEM, MEM_DIM))

    cos_p, sin_p = _rope_tables(jnp.arange(lp, dtype=F32))
    cos_s, sin_s = _rope_tables(PAST_LEN + jnp.arange(1, dtype=F32))
    bias = b_gate_mlstm[0]

    x = x_prompt.reshape(mp, D_MODEL)
    h = _rmsnorm(x, g_pre[0], "pre_norm_p0")
    pr = _in_proj(h, w_in[0], "p0")
    r3 = lambda a: a.reshape(bp, lp, a.shape[-1])
    u_main, ret_prompt = _ret_prompt(r3(pr["qk"]), r3(pr["v"]), r3(pr["z"]), cos_p, sin_p, g_head[0])
    u_mem = _memattn_prompt(r3(pr["qm"]), mem_k[0], mem_v[0], r3(pr["z"]))
    out = _outproj(u_main.reshape(mp, MAIN_DV), u_mem.reshape(mp, MEM_DIM), w_o[0], "outproj_p0")
    x, h = _post(out, x, g_post[0], g_pre[1], "post_p0")
    pr = _in_proj(h, w_in[1], "p1")
    gates_t = r3(pr["gates"][:, :2 * N_HEADS]).transpose(0, 2, 1)[:, :, None, :]
    u_main, c_prompt, n_prompt, m_prompt = _mlstm_prompt(
        r3(pr["qk"]), r3(pr["v"]), r3(pr["og"]), r3(pr["z"]), gates_t, bias, g_head[1])
    u_mem = _memattn_prompt(r3(pr["qm"]), mem_k[1], mem_v[1], r3(pr["z"]))
    out = _outproj(u_main.reshape(mp, MAIN_DV), u_mem.reshape(mp, MEM_DIM), w_o[1], "outproj_p1")
    y_prompt, _ = _post(out, x, g_post[1], None, "post_p1")

    xs = x_sample.reshape(bs, D_MODEL)
    h = _rmsnorm(xs, g_pre[0], "pre_norm_s0")
    pr = _in_proj(h, w_in[0], "s0")
    s3 = lambda a: a.reshape(bs, 1, a.shape[-1])
    gamma = 1.0 - 2.0 ** (-5.0 - jnp.arange(N_HEADS, dtype=F32))
    a_ret = jnp.exp(jnp.log(gamma))
    a_ret = jnp.broadcast_to(a_ret[None, :], (bs, N_HEADS)).reshape(-1)
    ones = jnp.ones((bs * N_HEADS,), F32)
    u_main, ret_sample = _decode(a_ret, ones, ones, s3(pr["qk"]), s3(pr["v"]), s3(pr["z"]),
                                 g_head[0], state_ret[0], rope_tabs=(cos_s, sin_s))
    ck = cache_mem_k.reshape(DEPTH, bs, N_MEM, MEM_DIM)
    cv = cache_mem_v.reshape(DEPTH, bs, N_MEM, MEM_DIM)
    u_mem = _memattn_decode(s3(pr["qm"]), ck[0], cv[0], s3(pr["z"]))
    out = _outproj(u_main.reshape(bs, MAIN_DV), u_mem.reshape(bs, MEM_DIM), w_o[0], "outproj_s0")
    xs, h = _post(out, xs, g_post[0], g_pre[1], "post_s0")
    pr = _in_proj(h, w_in[1], "s1")
    g_i = pr["gates"][:, :N_HEADS]
    g_f = pr["gates"][:, N_HEADS:2 * N_HEADS]
    m_new, a_m, w_m, e_m = _gate_decode(g_i, g_f, bias[:N_HEADS], bias[N_HEADS:], state_mlstm_m[0])
    u_main, c_sample, n_sample = _decode(
        a_m.reshape(-1), w_m.reshape(-1), e_m.reshape(-1), s3(pr["qk"]), s3(pr["v"]), s3(pr["z"]),
        g_head[1], state_mlstm_C[0], og=s3(pr["og"]),
        n_state=state_mlstm_n[0].reshape(bs, N_HEADS, 1, HEAD_DK))
    u_mem = _memattn_decode(s3(pr["qm"]), ck[1], cv[1], s3(pr["z"]))
    out = _outproj(u_main.reshape(bs, MAIN_DV), u_mem.reshape(bs, MEM_DIM), w_o[1], "outproj_s1")
    y_sample, _ = _post(out, xs, g_post[1], None, "post_s1")

    shape_kv = (DEPTH, bp, N_MEM, N_MEM_HEADS, MEM_HEAD_DIM)
    return (
        y_prompt.reshape(bp, lp, D_MODEL),
        y_sample.reshape(bs, 1, D_MODEL),
        ret_prompt[None],
        c_prompt[None],
        n_prompt.reshape(1, bp, N_HEADS, HEAD_DK),
        m_prompt[:, :, 0, 0][None],
        jnp.stack(mem_k).reshape(shape_kv),
        jnp.stack(mem_v).reshape(shape_kv),
        ret_sample[None],
        c_sample[None],
        n_sample.reshape(1, bs, N_HEADS, HEAD_DK),
        m_new[None],
    )
```

```python
import functools

import jax
import jax.numpy as jnp
from jax import lax
from jax.experimental import pallas as pl
from jax.experimental.pallas import tpu as pltpu

F32 = jnp.float32
BF16 = jnp.bfloat16

D_MODEL = 4096
DEPTH = 2
PAST_LEN = 16384
MIX_WIDTH = 2 * D_MODEL
N_HEADS = 12
HEAD_DV = MIX_WIDTH // 16
HEAD_DK = HEAD_DV // 2
MAIN_DK = N_HEADS * HEAD_DK
MAIN_DV = N_HEADS * HEAD_DV
N_MEM = 256
N_MEM_HEADS = 4
MEM_HEAD_DIM = MIX_WIDTH // 16
MEM_DIM = N_MEM_HEADS * MEM_HEAD_DIM
CHUNK = 128
ROPE_THETA = 10000.0
EPS = 1e-6
NEG_INF = -1e30

COL_QK = 0
COL_V = 2 * MAIN_DK
COL_QM = COL_V + MAIN_DV
COL_Z = COL_QM + MEM_DIM
COL_OG = COL_Z + MIX_WIDTH
COL_GATES = COL_OG + MAIN_DV

VMEM_LIMIT_BYTES = 56 * 1024 * 1024
BF16_SUBLANES = 16
MM_MAX_TILE_M = 1040
MM_TILE_N = 512
OUT_TILE_M = 512
SCAN_TILE_L = 512
SCAN_HEADS = 2
ROW_TILE = 128

_NT = (((1,), (1,)), ((), ()))
_TN = (((0,), (0,)), ((), ()))


def _cparams(sem):
    return pltpu.CompilerParams(dimension_semantics=sem, vmem_limit_bytes=VMEM_LIMIT_BYTES)


def _row_tile(m, limit):
    best = None
    for t in range(BF16_SUBLANES, limit + 1, BF16_SUBLANES):
        if m % t == 0:
            best = t
    assert best is not None, m
    return best


def _sigmoid(x):
    return 0.5 * jnp.tanh(0.5 * x) + 0.5


def _log_sigmoid(x):
    return jnp.minimum(x, 0.0) - jnp.log(1.0 + jnp.exp(-jnp.abs(x)))


def _rope(x, cos, sin):
    half = HEAD_DK // 2
    x1, x2 = x[:, :half], x[:, half:]
    return jnp.concatenate([x1 * cos - x2 * sin, x1 * sin + x2 * cos], axis=-1)


def _head_norm_gate(main, g, z):
    mu = jnp.mean(main, axis=-1, keepdims=True)
    d = main - mu
    var = jnp.mean(d * d, axis=-1, keepdims=True)
    hn = d * lax.rsqrt(var + EPS)
    return hn * g * (z * _sigmoid(z))


def _rms(x, g):
    ms = jnp.mean(x * x, axis=-1, keepdims=True)
    return x * lax.rsqrt(ms + EPS) * g


def _dot(a, b):
    return jnp.dot(a, b, preferred_element_type=F32)


def _rmsnorm_body(x_ref, g_ref, o_ref):
    o_ref[...] = _rms(x_ref[...], g_ref[...]).astype(o_ref.dtype)


def _rmsnorm(x, g, name):
    m, d = x.shape
    tm = ROW_TILE
    return pl.pallas_call(
        _rmsnorm_body,
        grid=(m // tm,),
        in_specs=[pl.BlockSpec((tm, d), lambda i: (i, 0)),
                  pl.BlockSpec((1, d), lambda i: (0, 0))],
        out_specs=pl.BlockSpec((tm, d), lambda i: (i, 0)),
        out_shape=jax.ShapeDtypeStruct((m, d), BF16),
        compiler_params=_cparams(("parallel",)),
        name=name,
    )(x, g.reshape(1, d))


def _rmsnorm_stacked_body(xp_ref, xs_ref, g_ref, o_ref, *, n_prompt_tiles):
    i = pl.program_id(0)

    @pl.when(i < n_prompt_tiles)
    def _():
        o_ref[...] = _rms(xp_ref[...], g_ref[...]).astype(o_ref.dtype)

    @pl.when(i >= n_prompt_tiles)
    def _():
        o_ref[...] = _rms(xs_ref[...], g_ref[...]).astype(o_ref.dtype)


def _rmsnorm_stacked(xp, xs, g):
    mp, d = xp.shape
    ms = xs.shape[0]
    tm = ROW_TILE
    npt, nst = mp // tm, ms // tm
    return pl.pallas_call(
        functools.partial(_rmsnorm_stacked_body, n_prompt_tiles=npt),
        grid=(npt + nst,),
        in_specs=[pl.BlockSpec((tm, d), lambda i: (jnp.minimum(i, npt - 1), 0)),
                  pl.BlockSpec((tm, d), lambda i: (jnp.maximum(i - npt, 0), 0)),
                  pl.BlockSpec((1, d), lambda i: (0, 0))],
        out_specs=pl.BlockSpec((tm, d), lambda i: (i, 0)),
        out_shape=jax.ShapeDtypeStruct((mp + ms, d), BF16),
        compiler_params=_cparams(("arbitrary",)),
        name="pre_norm_0",
    )(xp, xs, g.reshape(1, d))


def _mm_w32_body(a_ref, w_ref, o_ref, wb_ref):
    @pl.when(pl.program_id(1) == 0)
    def _():
        wb_ref[...] = w_ref[...].astype(BF16)

    o_ref[...] = _dot(a_ref[...], wb_ref[...]).astype(o_ref.dtype)


def _matmul_w32(a, w, layer, col0, ncols, out_dtype, name):
    m, k = a.shape
    tm = _row_tile(m, MM_MAX_TILE_M)
    tn = min(MM_TILE_N, ncols)
    assert col0 % tn == 0 and ncols % tn == 0
    off = col0 // tn
    return pl.pallas_call(
        _mm_w32_body,
        grid=(ncols // tn, m // tm),
        in_specs=[pl.BlockSpec((tm, k), lambda j, i: (i, 0)),
                  pl.BlockSpec((None, k, tn), lambda j, i: (layer, 0, off + j))],
        out_specs=pl.BlockSpec((tm, tn), lambda j, i: (i, j)),
        out_shape=jax.ShapeDtypeStruct((m, ncols), out_dtype),
        scratch_shapes=[pltpu.VMEM((k, tn), BF16)],
        compiler_params=_cparams(("parallel", "arbitrary")),
        name=name,
    )(a, w)


def _mm_body(a_ref, w_ref, o_ref):
    o_ref[...] = _dot(a_ref[...], w_ref[...]).astype(o_ref.dtype)


def _matmul(a, w, out_dtype, name):
    m, k = a.shape
    n = w.shape[1]
    tm = _row_tile(m, MM_MAX_TILE_M)
    return pl.pallas_call(
        _mm_body,
        grid=(m // tm,),
        in_specs=[pl.BlockSpec((tm, k), lambda i: (i, 0)),
                  pl.BlockSpec((k, n), lambda i: (0, 0))],
        out_specs=pl.BlockSpec((tm, n), lambda i: (i, 0)),
        out_shape=jax.ShapeDtypeStruct((m, n), out_dtype),
        compiler_params=_cparams(("parallel",)),
        name=name,
    )(a, w)


def _outproj_body(um_ref, ue_ref, w_ref, o_ref):
    o_ref[...] = (_dot(um_ref[...], w_ref[:MAIN_DV, :]) + _dot(ue_ref[...], w_ref[MAIN_DV:, :]))


def _outproj(u_main, u_mem, w, layer, name):
    m = u_main.shape[0]
    n = w.shape[2]
    tm = _row_tile(m, OUT_TILE_M)
    tn = MM_TILE_N
    return pl.pallas_call(
        _outproj_body,
        grid=(m // tm, n // tn),
        in_specs=[pl.BlockSpec((tm, MAIN_DV), lambda i, j: (i, 0)),
                  pl.BlockSpec((tm, MEM_DIM), lambda i, j: (i, 0)),
                  pl.BlockSpec((None, MIX_WIDTH, tn), lambda i, j: (layer, 0, j))],
        out_specs=pl.BlockSpec((tm, tn), lambda i, j: (i, j)),
        out_shape=jax.ShapeDtypeStruct((m, n), F32),
        compiler_params=_cparams(("parallel", "parallel")),
        name=name,
    )(u_main, u_mem, w)


def _post_mid_body(op_ref, os_ref, xp_ref, xs_ref, gp_ref, gn_ref, y_ref, h_ref, *, n_prompt_tiles):
    i = pl.program_id(0)

    def emit(o_ref, x_ref):
        y = x_ref[...] + _rms(o_ref[...], gp_ref[...])
        y_ref[...] = y
        h_ref[...] = _rms(y, gn_ref[...]).astype(h_ref.dtype)

    @pl.when(i < n_prompt_tiles)
    def _():
        emit(op_ref, xp_ref)

    @pl.when(i >= n_prompt_tiles)
    def _():
        emit(os_ref, xs_ref)


def _post_mid(out_p, out_s, x_p, x_s, g_post, g_next):
    mp, d = x_p.shape
    ms = x_s.shape[0]
    tm = ROW_TILE
    npt, nst = mp // tm, ms // tm
    prow = pl.BlockSpec((tm, d), lambda i: (jnp.minimum(i, npt - 1), 0))
    srow = pl.BlockSpec((tm, d), lambda i: (jnp.maximum(i - npt, 0), 0))
    vec = pl.BlockSpec((1, d), lambda i: (0, 0))
    orow = pl.BlockSpec((tm, d), lambda i: (i, 0))
    return pl.pallas_call(
        functools.partial(_post_mid_body, n_prompt_tiles=npt),
        grid=(npt + nst,),
        in_specs=[prow, srow, prow, srow, vec, vec],
        out_specs=[orow, orow],
        out_shape=[jax.ShapeDtypeStruct((mp + ms, d), F32),
                   jax.ShapeDtypeStruct((mp + ms, d), BF16)],
        compiler_params=_cparams(("arbitrary",)),
        name="post_0",
    )(out_p, out_s, x_p, x_s, g_post.reshape(1, d), g_next.reshape(1, d))


def _post_last_body(o_ref, x_ref, gp_ref, y_ref):
    y_ref[...] = x_ref[...] + _rms(o_ref[...], gp_ref[...])


def _post_last(out, x_all, row0, g_post, name):
    m, d = out.shape
    tm = ROW_TILE
    off = row0 // tm
    return pl.pallas_call(
        _post_last_body, grid=(m // tm,),
        in_specs=[pl.BlockSpec((tm, d), lambda i: (i, 0)),
                  pl.BlockSpec((tm, d), lambda i: (off + i, 0)),
                  pl.BlockSpec((1, d), lambda i: (0, 0))],
        out_specs=pl.BlockSpec((tm, d), lambda i: (i, 0)),
        out_shape=jax.ShapeDtypeStruct((m, d), F32),
        compiler_params=_cparams(("parallel",)), name=name,
    )(out, x_all, g_post.reshape(1, d))


def _ret_prompt_body(q_ref, k_ref, v_ref, z_ref, cos_ref, sin_ref, din_ref, qd_ref, kd_ref,
                     cd_ref, g_ref, u_ref, s_ref):
    @pl.when(pl.program_id(2) == 0)
    def _():
        s_ref[...] = jnp.zeros_like(s_ref)

    for ci in range(SCAN_TILE_L // CHUNK):
        sl = slice(ci * CHUNK, (ci + 1) * CHUNK)
        cos, sin = cos_ref[sl, :], sin_ref[sl, :]
        for hh in range(SCAN_HEADS):
            ks = slice(hh * HEAD_DK, (hh + 1) * HEAD_DK)
            vs = slice(hh * HEAD_DV, (hh + 1) * HEAD_DV)
            din = din_ref[hh]
            qd = qd_ref[hh]
            kd = kd_ref[hh]
            cd = cd_ref[hh][:, :1]
            q = _rope(q_ref[sl, ks], cos, sin)
            k = _rope(k_ref[sl, ks], cos, sin) * HEAD_DK ** -0.5
            v = v_ref[sl, vs]
            s = s_ref[0, hh]
            sc = lax.dot_general(q.astype(BF16), k.astype(BF16), _NT,
                                 preferred_element_type=F32) * din
            o = _dot(sc.astype(BF16), v) + _dot((q * qd).astype(BF16), s.astype(BF16))
            s_ref[0, hh] = s * cd + lax.dot_general((k * kd).astype(BF16), v, _TN,
                                                    preferred_element_type=F32)
            u_ref[sl, vs] = _head_norm_gate(o, g_ref[:, vs], z_ref[sl, vs]).astype(u_ref.dtype)


def _ret_prompt(qk, v, z, cos, sin, g_head, b, l):
    tl, c, hb = SCAN_TILE_L, CHUNK, SCAN_HEADS
    nt = l // tl
    lg = jnp.log(1.0 - 2.0 ** (-5.0 - jnp.arange(N_HEADS, dtype=F32)))
    idx = jnp.arange(c, dtype=F32)
    diff = idx[:, None] - idx[None, :]
    decay_in = jnp.where(diff[None] >= 0.0,
                         jnp.exp(lg[:, None, None] * jnp.maximum(diff, 0.0)[None]), 0.0)
    q_dec = jnp.exp(lg[:, None] * (idx + 1.0)[None, :])[:, :, None]
    k_dec = jnp.exp(lg[:, None] * (c - 1.0 - idx)[None, :])[:, :, None]
    chunk_dec = jnp.broadcast_to(jnp.exp(lg * c)[:, None, None], (N_HEADS, 1, 128))
    nkb = MAIN_DK // (hb * HEAD_DK)
    return pl.pallas_call(
        _ret_prompt_body,
        grid=(b, N_HEADS // hb, nt),
        in_specs=[
            pl.BlockSpec((tl, hb * HEAD_DK), lambda i, h, t: (i * nt + t, h)),
            pl.BlockSpec((tl, hb * HEAD_DK), lambda i, h, t: (i * nt + t, nkb + h)),
            pl.BlockSpec((tl, hb * HEAD_DV), lambda i, h, t: (i * nt + t, h)),
            pl.BlockSpec((tl, hb * HEAD_DV), lambda i, h, t: (i * nt + t, h)),
            pl.BlockSpec((tl, HEAD_DK // 2), lambda i, h, t: (t, 0)),
            pl.BlockSpec((tl, HEAD_DK // 2), lambda i, h, t: (t, 0)),
            pl.BlockSpec((hb, c, c), lambda i, h, t: (h, 0, 0)),
            pl.BlockSpec((hb, c, 1), lambda i, h, t: (h, 0, 0)),
            pl.BlockSpec((hb, c, 1), lambda i, h, t: (h, 0, 0)),
            pl.BlockSpec((hb, 1, 128), lambda i, h, t: (h, 0, 0)),
            pl.BlockSpec((1, hb * HEAD_DV), lambda i, h, t: (0, h)),
        ],
        out_specs=[
            pl.BlockSpec((tl, hb * HEAD_DV), lambda i, h, t: (i * nt + t, h)),
            pl.BlockSpec((1, hb, HEAD_DK, HEAD_DV), lambda i, h, t: (i, h, 0, 0)),
        ],
        out_shape=[jax.ShapeDtypeStruct((b * l, MAIN_DV), BF16),
                   jax.ShapeDtypeStruct((b, N_HEADS, HEAD_DK, HEAD_DV), F32)],
        compiler_params=_cparams(("parallel", "parallel", "arbitrary")),
        name="ret_prompt",
    )(qk, qk, v, z, cos, sin, decay_in, q_dec, k_dec, chunk_dec, g_head.reshape(1, MAIN_DV))


def _col_from_row(row, eye):
    return jnp.sum(jnp.where(eye, row, 0.0), axis=1, keepdims=True)


def _mlstm_prompt_body(q_ref, k_ref, v_ref, og_ref, z_ref, gi_ref, gf_ref, bi_ref, bf_ref, g_ref,
                       u_ref, c_ref, n_ref, m_ref):
    @pl.when(pl.program_id(2) == 0)
    def _():
        c_ref[...] = jnp.zeros_like(c_ref)
        n_ref[...] = jnp.zeros_like(n_ref)
        m_ref[...] = jnp.full_like(m_ref, NEG_INF)

    c = CHUNK
    ri = lax.broadcasted_iota(jnp.int32, (c, c), 0)
    cj = lax.broadcasted_iota(jnp.int32, (c, c), 1)
    eye = ri == cj
    causal = ri >= cj
    lane8 = lax.broadcasted_iota(jnp.int32, (8, c), 1)
    for ci in range(SCAN_TILE_L // CHUNK):
        sl = slice(ci * CHUNK, (ci + 1) * CHUNK)
        for hh in range(SCAN_HEADS):
            ks = slice(hh * HEAD_DK, (hh + 1) * HEAD_DK)
            vs = slice(hh * HEAD_DV, (hh + 1) * HEAD_DV)
            ic = gi_ref[0, hh, :, sl] + bi_ref[hh]
            lf = _log_sigmoid(gf_ref[0, hh, :, sl] + bf_ref[hh])
            cs = jnp.broadcast_to(lf, (8, c))
            sh = 1
            while sh < c:
                cs = cs + jnp.where(lane8 >= sh, pltpu.roll(cs, sh, 1), 0.0)
                sh *= 2
            b_row = cs[:1, :]
            r_row = ic - b_row
            b_col = _col_from_row(b_row, eye)
            r_col = _col_from_row(r_row, eye)
            b_last = b_row[:, c - 1:]
            m_prev = m_ref[0, hh][:, :1]

            logw = jnp.where(causal, b_col + r_row, NEG_INF)
            log_prev = b_col + m_prev
            m_t = jnp.maximum(log_prev, jnp.max(logw, axis=1, keepdims=True))
            w = jnp.exp(logw - m_t)
            a_prev = jnp.exp(log_prev - m_t)

            q = q_ref[sl, ks]
            k = k_ref[sl, ks] * HEAD_DK ** -0.5
            v = v_ref[sl, vs]
            qb, kb = q.astype(BF16), k.astype(BF16)
            cm = c_ref[0, hh]
            nv = n_ref[0, hh]
            sc = lax.dot_general(qb, kb, _NT, preferred_element_type=F32) * w
            num = _dot(sc.astype(BF16), v) + a_prev * _dot(qb, cm.astype(BF16))
            qn = jnp.sum(qb.astype(F32) * nv.astype(BF16).astype(F32), axis=1, keepdims=True)
            den = jnp.sum(sc, axis=1, keepdims=True) + a_prev * qn
            hc = num / jnp.maximum(jnp.abs(den), jnp.exp(-m_t))

            m_new = m_t[c - 1:, :]
            a_c = jnp.exp(b_last + m_prev - m_new)
            wk_col = jnp.exp(b_last + r_col - m_new)
            wk_row = jnp.exp(b_last + r_row - m_new)
            c_ref[0, hh] = a_c * cm + lax.dot_general((k * wk_col).astype(BF16), v, _TN,
                                                      preferred_element_type=F32)
            wk8 = jnp.broadcast_to(wk_row, (8, c)).astype(BF16)
            n_ref[0, hh] = a_c * nv + _dot(wk8, kb)[:1, :]
            m_ref[0, hh] = jnp.broadcast_to(m_new, (1, 128))

            main = hc * _sigmoid(og_ref[sl, vs])
            u_ref[sl, vs] = _head_norm_gate(main, g_ref[:, vs], z_ref[sl, vs]).astype(u_ref.dtype)


def _mlstm_prompt(qk, v, og, z, gates_t, bias, g_head, b, l):
    tl, hb = SCAN_TILE_L, SCAN_HEADS
    nt = l // tl
    bias3 = bias.reshape(2 * N_HEADS, 1, 1)
    nkb = MAIN_DK // (hb * HEAD_DK)
    nhb = N_HEADS // hb
    dk_spec = lambda off: pl.BlockSpec((tl, hb * HEAD_DK), lambda i, h, t: (i * nt + t, off + h))
    dv_spec = pl.BlockSpec((tl, hb * HEAD_DV), lambda i, h, t: (i * nt + t, h))
    return pl.pallas_call(
        _mlstm_prompt_body,
        grid=(b, nhb, nt),
        in_specs=[
            dk_spec(0), dk_spec(nkb), dv_spec, dv_spec, dv_spec,
            pl.BlockSpec((1, hb, 1, tl), lambda i, h, t: (i, h, 0, t)),
            pl.BlockSpec((1, hb, 1, tl), lambda i, h, t: (i, nhb + h, 0, t)),
            pl.BlockSpec((hb, 1, 1), lambda i, h, t: (h, 0, 0)),
            pl.BlockSpec((hb, 1, 1), lambda i, h, t: (nhb + h, 0, 0)),
            pl.BlockSpec((1, hb * HEAD_DV), lambda i, h, t: (0, h)),
        ],
        out_specs=[
            dv_spec,
            pl.BlockSpec((1, hb, HEAD_DK, HEAD_DV), lambda i, h, t: (i, h, 0, 0)),
            pl.BlockSpec((1, hb, 1, HEAD_DK), lambda i, h, t: (i, h, 0, 0)),
            pl.BlockSpec((1, hb, 1, 128), lambda i, h, t: (i, h, 0, 0)),
        ],
        out_shape=[jax.ShapeDtypeStruct((b * l, MAIN_DV), BF16),
                   jax.ShapeDtypeStruct((b, N_HEADS, HEAD_DK, HEAD_DV), F32),
                   jax.ShapeDtypeStruct((b, N_HEADS, 1, HEAD_DK), F32),
                   jax.ShapeDtypeStruct((b, N_HEADS, 1, 128), F32)],
        compiler_params=_cparams(("parallel", "parallel", "arbitrary")),
        name="mlstm_prompt",
    )(qk, qk, v, og, z, gates_t, gates_t, bias3, bias3, g_head.reshape(1, MAIN_DV))


def _memattn_prompt_body(q_ref, k_ref, v_ref, z_ref, u_ref):
    q = (q_ref[...] * MEM_HEAD_DIM ** -0.5).astype(BF16)
    s = lax.dot_general(q, k_ref[...].astype(BF16), _NT, preferred_element_type=F32)
    p = jnp.exp(s - jnp.max(s, axis=-1, keepdims=True))
    p = p / jnp.sum(p, axis=-1, keepdims=True)
    o = _dot(p.astype(BF16), v_ref[...].astype(BF16))
    z = z_ref[...]
    u_ref[...] = (o * (z * _sigmoid(z))).astype(u_ref.dtype)


def _memattn_prompt(qm, mk, mv, z, b, l):
    tl = SCAN_TILE_L
    nt = l // tl
    zoff = MAIN_DV // MEM_HEAD_DIM
    return pl.pallas_call(
        _memattn_prompt_body,
        grid=(b, N_MEM_HEADS, nt),
        in_specs=[
            pl.BlockSpec((tl, MEM_HEAD_DIM), lambda i, h, t: (i * nt + t, h)),
            pl.BlockSpec((N_MEM, MEM_HEAD_DIM), lambda i, h, t: (i, h)),
            pl.BlockSpec((N_MEM, MEM_HEAD_DIM), lambda i, h, t: (i, h)),
            pl.BlockSpec((tl, MEM_HEAD_DIM), lambda i, h, t: (i * nt + t, zoff + h)),
        ],
        out_specs=pl.BlockSpec((tl, MEM_HEAD_DIM), lambda i, h, t: (i * nt + t, h)),
        out_shape=jax.ShapeDtypeStruct((b * l, MEM_DIM), BF16),
        compiler_params=_cparams(("parallel", "parallel", "parallel")),
        name="memattn_prompt",
    )(qm, mk, mv, z)


def _col_bcast(row16, width):
    ones = jnp.ones((BF16_SUBLANES, width), BF16)
    return lax.dot_general(row16, ones, _TN, preferred_element_type=F32)


def _first_row16(x):
    n = x.shape[1]
    r = lax.broadcasted_iota(jnp.int32, (BF16_SUBLANES, n), 0)
    return jnp.where(r == 0, jnp.broadcast_to(x, (BF16_SUBLANES, n)), 0.0).astype(BF16)


def _decode_body(a_ref, w_ref, e_ref, qk_ref, v_ref, z_ref, g_ref, s_ref, *rest,
                 use_rope, is_mlstm):
    if use_rope:
        cos_ref, sin_ref = rest[:2]
        rest = rest[2:]
    if is_mlstm:
        og_ref, n_ref, u_ref, so_ref, no_ref = rest
    else:
        u_ref, so_ref = rest
    base = pl.program_id(0) * N_HEADS
    for h in range(N_HEADS):
        a = a_ref[base + h]
        w = w_ref[base + h]
        qs = slice(h * HEAD_DK, (h + 1) * HEAD_DK)
        ks = slice(MAIN_DK + h * HEAD_DK, MAIN_DK + (h + 1) * HEAD_DK)
        vs = slice(h * HEAD_DV, (h + 1) * HEAD_DV)
        q = qk_ref[0, :, qs]
        k = qk_ref[0, :, ks]
        if use_rope:
            q = _rope(q, cos_ref[...], sin_ref[...])
            k = _rope(k, cos_ref[...], sin_ref[...])
        k = k * HEAD_DK ** -0.5
        v = v_ref[0, :, vs]
        vf = v.astype(F32)
        qb = q.astype(BF16)
        kb = k.astype(BF16)
        s = s_ref[0, h]

        qk = jnp.sum(qb.astype(F32) * kb.astype(F32), axis=1, keepdims=True)
        sc = qk * w
        q_cols = _col_bcast(_first_row16(q), HEAD_DV)
        qs_ = jnp.sum(q_cols * s, axis=0, keepdims=True)
        num = sc.astype(BF16).astype(F32) * vf + a * qs_
        kv = lax.dot_general(_first_row16(k * w), jnp.broadcast_to(v, (BF16_SUBLANES, HEAD_DV)),
                             _TN, preferred_element_type=F32)
        so_ref[0, h] = a * s + kv

        if is_mlstm:
            nv = n_ref[0, h]
            qn = jnp.sum(qb.astype(F32) * nv.astype(BF16).astype(F32), axis=1, keepdims=True)
            den = sc + a * qn
            main = num / jnp.maximum(jnp.abs(den), e_ref[base + h])
            wb = (jnp.zeros((1, 1), F32) + w).astype(BF16).astype(F32)
            no_ref[0, h] = a * nv + wb * kb.astype(F32)
            main = main * _sigmoid(og_ref[0, :, vs])
        else:
            main = num
        u_ref[0, :, vs] = _head_norm_gate(main, g_ref[:, vs], z_ref[0, :, vs]).astype(u_ref.dtype)


def _decode(a, w, e, qk, v, z, g_head, state, rope_tabs=None, og=None, n_state=None):
    b = qk.shape[0]
    is_mlstm = og is not None
    use_rope = rope_tabs is not None
    qk_spec = pl.BlockSpec((1, 1, 2 * MAIN_DK), lambda i, *_: (i, 0, 0))
    dv_spec = pl.BlockSpec((1, 1, MAIN_DV), lambda i, *_: (i, 0, 0))
    st_spec = pl.BlockSpec((1, N_HEADS, HEAD_DK, HEAD_DV), lambda i, *_: (i, 0, 0, 0))
    n_spec = pl.BlockSpec((1, N_HEADS, 1, HEAD_DK), lambda i, *_: (i, 0, 0, 0))
    in_specs = [qk_spec, dv_spec, dv_spec,
                pl.BlockSpec((1, MAIN_DV), lambda i, *_: (0, 0)), st_spec]
    args = [qk, v, z, g_head.reshape(1, MAIN_DV), state]
    out_specs = [dv_spec, st_spec]
    out_shape = [jax.ShapeDtypeStruct((b, 1, MAIN_DV), BF16),
                 jax.ShapeDtypeStruct(state.shape, F32)]
    if use_rope:
        tab = pl.BlockSpec((1, HEAD_DK // 2), lambda i, *_: (0, 0))
        in_specs += [tab, tab]
        args += list(rope_tabs)
    if is_mlstm:
        in_specs += [dv_spec, n_spec]
        args += [og, n_state]
        out_specs.append(n_spec)
        out_shape.append(jax.ShapeDtypeStruct(n_state.shape, F32))
    return pl.pallas_call(
        functools.partial(_decode_body, use_rope=use_rope, is_mlstm=is_mlstm),
        grid_spec=pltpu.PrefetchScalarGridSpec(
            num_scalar_prefetch=3, grid=(b,),
            in_specs=in_specs, out_specs=out_specs),
        out_shape=out_shape,
        compiler_params=_cparams(("parallel",)),
        name="mlstm_decode" if is_mlstm else "ret_decode",
    )(a, w, e, *args)


def _gate_decode_body(gi_ref, gf_ref, bi_ref, bf_ref, m_ref, mo_ref, a_ref, w_ref, e_ref):
    ic = gi_ref[...] + bi_ref[...]
    lf = _log_sigmoid(gf_ref[...] + bf_ref[...])
    log_prev = lf + m_ref[...]
    m_t = jnp.maximum(log_prev, ic)
    mo_ref[...] = m_t
    a_ref[...] = jnp.exp(log_prev - m_t)
    w_ref[...] = jnp.exp(ic - m_t)
    e_ref[...] = jnp.exp(-m_t)


def _gate_decode(g_i, g_f, b_i, b_f, m):
    sds = jax.ShapeDtypeStruct(m.shape, F32)
    return pl.pallas_call(
        _gate_decode_body, out_shape=[sds, sds, sds, sds], name="mlstm_decode_gates",
    )(g_i, g_f, b_i.reshape(1, N_HEADS), b_f.reshape(1, N_HEADS), m)


def _memattn_decode_body(q_ref, k_ref, v_ref, z_ref, u_ref):
    q = q_ref[0] * MEM_HEAD_DIM ** -0.5
    s = jnp.sum(k_ref[...] * q[None], axis=-1, keepdims=True)
    p = jnp.exp(s - jnp.max(s, axis=0, keepdims=True))
    p = p / jnp.sum(p, axis=0, keepdims=True)
    o = jnp.sum(p * v_ref[...], axis=0)
    z = z_ref[0]
    u_ref[0] = (o * (z * _sigmoid(z))).astype(u_ref.dtype)


def _memattn_decode(qm, cache_k, cache_v, z, layer):
    b = qm.shape[0]
    row = pl.BlockSpec((1, N_MEM_HEADS, MEM_HEAD_DIM), lambda i: (i, 0, 0))
    kv = pl.BlockSpec((None, None, N_MEM, N_MEM_HEADS, MEM_HEAD_DIM),
                      lambda i: (layer, i, 0, 0, 0))
    return pl.pallas_call(
        _memattn_decode_body,
        grid=(b,),
        in_specs=[row, kv, kv, row],
        out_specs=row,
        out_shape=jax.ShapeDtypeStruct((b, N_MEM_HEADS, MEM_HEAD_DIM), BF16),
        compiler_params=_cparams(("parallel",)),
        name=f"memattn_decode_{layer}",
    )(qm, cache_k, cache_v, z)


def _rope_tables(pos):
    d = HEAD_DK
    inv = 1.0 / (ROPE_THETA ** (jnp.arange(0, d, 2, dtype=F32) / d))
    ang = pos[:, None] * inv[None, :]
    return jnp.cos(ang), jnp.sin(ang)


_REGIONS = (("qk", COL_QK, COL_V, F32), ("v", COL_V, COL_QM, BF16), ("qm", COL_QM, COL_Z, F32),
            ("z", COL_Z, COL_OG, F32), ("og", COL_OG, COL_GATES, F32))


def _in_proj(h, w, tag, with_mlstm):
    out = {}
    for name, c0, c1, dt in _REGIONS:
        if name == "og" and not with_mlstm:
            continue
        out[name] = _matmul_w32(h, w, 0, c0, c1 - c0, dt, f"inproj_{name}_{tag}")
    if with_mlstm:
        wg = jnp.pad(w[0, :, COL_GATES:], ((0, 0), (0, 128 - 2 * N_HEADS))).astype(BF16)
        out["gates"] = _matmul(h, wg, F32, f"inproj_gates_{tag}")
    return out


def kernel(x_prompt, x_sample, mem_prompt, state_ret, state_mlstm_C, state_mlstm_n, state_mlstm_m,
           cache_mem_k, cache_mem_v, g_pre, g_post, w_in_ret, w_in_mlstm, b_gate_mlstm, g_head,
           w_out, g_mem, w_mem_kv):
    bp, lp, _ = x_prompt.shape
    bs = x_sample.shape[0]
    mp = bp * lp

    w_o = w_out.astype(BF16)

    mem2d = mem_prompt.reshape(bp * N_MEM, D_MODEL)
    mem_k, mem_v = [], []
    for l in range(DEPTH):
        hm = _rmsnorm(mem2d, g_mem[l], f"mem_norm_{l}")
        mem_k.append(_matmul_w32(hm, w_mem_kv, l, 0, MEM_DIM, F32, f"mem_k_{l}"))
        mem_v.append(_matmul_w32(hm, w_mem_kv, l, MEM_DIM, MEM_DIM, F32, f"mem_v_{l}"))

    cos_p, sin_p = _rope_tables(jnp.arange(lp, dtype=F32))
    cos_s, sin_s = _rope_tables(PAST_LEN + jnp.arange(1, dtype=F32))
    bias = b_gate_mlstm[0]
    xp = x_prompt.reshape(mp, D_MODEL)
    xs = x_sample.reshape(bs, D_MODEL)
    srow = lambda a: a[mp:].reshape(bs, 1, a.shape[-1])
    smem = lambda a: a[mp:].reshape(bs, N_MEM_HEADS, MEM_HEAD_DIM)

    h = _rmsnorm_stacked(xp, xs, g_pre[0])
    pr = _in_proj(h, w_in_ret, "0", False)
    u_main_p, ret_prompt = _ret_prompt(pr["qk"], pr["v"], pr["z"], cos_p, sin_p, g_head[0], bp, lp)
    u_mem_p = _memattn_prompt(pr["qm"], mem_k[0], mem_v[0], pr["z"], bp, lp)
    out_p = _outproj(u_main_p, u_mem_p, w_o, 0, "outproj_p0")

    gamma = 1.0 - 2.0 ** (-5.0 - jnp.arange(N_HEADS, dtype=F32))
    a_ret = jnp.exp(jnp.log(gamma))
    a_ret = jnp.broadcast_to(a_ret[None, :], (bs, N_HEADS)).reshape(-1)
    ones = jnp.ones((bs * N_HEADS,), F32)
    u_main_s, ret_sample = _decode(a_ret, ones, ones, srow(pr["qk"]), srow(pr["v"]), srow(pr["z"]),
                                   g_head[0], state_ret[0], rope_tabs=(cos_s, sin_s))
    u_mem_s = _memattn_decode(smem(pr["qm"]), cache_mem_k, cache_mem_v,
                              smem(pr["z"][:, MAIN_DV:]), 0)
    out_s = _outproj(u_main_s.reshape(bs, MAIN_DV), u_mem_s.reshape(bs, MEM_DIM), w_o, 0,
                     "outproj_s0")
    x1, h = _post_mid(out_p, out_s, xp, xs, g_post[0], g_pre[1])

    pr = _in_proj(h, w_in_mlstm, "1", True)
    gates = pr["gates"][:, :2 * N_HEADS]
    gates_t = gates[:mp].reshape(bp, lp, 2 * N_HEADS).transpose(0, 2, 1)[:, :, None, :]
    u_main_p, c_prompt, n_prompt, m_prompt = _mlstm_prompt(
        pr["qk"], pr["v"], pr["og"], pr["z"], gates_t, bias, g_head[1], bp, lp)
    u_mem_p = _memattn_prompt(pr["qm"], mem_k[1], mem_v[1], pr["z"], bp, lp)
    out_p = _outproj(u_main_p, u_mem_p, w_o, 1, "outproj_p1")
    y_prompt = _post_last(out_p, x1, 0, g_post[1], "post_p1")

    g_i = gates[mp:, :N_HEADS]
    g_f = gates[mp:, N_HEADS:]
    m_new, a_m, w_m, e_m = _gate_decode(g_i, g_f, bias[:N_HEADS], bias[N_HEADS:], state_mlstm_m[0])
    u_main_s, c_sample, n_sample = _decode(
        a_m.reshape(-1), w_m.reshape(-1), e_m.reshape(-1), srow(pr["qk"]), srow(pr["v"]),
        srow(pr["z"]), g_head[1], state_mlstm_C[0], og=srow(pr["og"]),
        n_state=state_mlstm_n[0].reshape(bs, N_HEADS, 1, HEAD_DK))
    u_mem_s = _memattn_decode(smem(pr["qm"]), cache_mem_k, cache_mem_v,
                              smem(pr["z"][:, MAIN_DV:]), 1)
    out_s = _outproj(u_main_s.reshape(bs, MAIN_DV), u_mem_s.reshape(bs, MEM_DIM), w_o, 1,
                     "outproj_s1")
    y_sample = _post_last(out_s, x1, mp, g_post[1], "post_s1")

    shape_kv = (DEPTH, bp, N_MEM, N_MEM_HEADS, MEM_HEAD_DIM)
    return (
        y_prompt.reshape(bp, lp, D_MODEL),
        y_sample.reshape(bs, 1, D_MODEL),
        ret_prompt[None],
        c_prompt[None],
        n_prompt.reshape(1, bp, N_HEADS, HEAD_DK),
        m_prompt[:, :, 0, 0][None],
        jnp.stack(mem_k).reshape(shape_kv),
        jnp.stack(mem_v).reshape(shape_kv),
        ret_sample[None],
        c_sample[None],
        n_sample.reshape(1, bs, N_HEADS, HEAD_DK),
        m_new[None],
    )
```

```python
import functools

import jax
import jax.numpy as jnp
from jax import lax
from jax.experimental import pallas as pl
from jax.experimental.pallas import tpu as pltpu

F32 = jnp.float32
BF16 = jnp.bfloat16

D_MODEL = 4096
DEPTH = 2
PAST_LEN = 16384
MIX_WIDTH = 2 * D_MODEL
N_HEADS = 12
HEAD_DV = MIX_WIDTH // 16
HEAD_DK = HEAD_DV // 2
MAIN_DK = N_HEADS * HEAD_DK
MAIN_DV = N_HEADS * HEAD_DV
N_MEM = 256
N_MEM_HEADS = 4
MEM_HEAD_DIM = MIX_WIDTH // 16
MEM_DIM = N_MEM_HEADS * MEM_HEAD_DIM
CHUNK = 128
ROPE_THETA = 10000.0
EPS = 1e-6
NEG_INF = -1e30

COL_QK = 0
COL_V = 2 * MAIN_DK
COL_QM = COL_V + MAIN_DV
COL_Z = COL_QM + MEM_DIM
COL_OG = COL_Z + MIX_WIDTH
COL_GATES = COL_OG + MAIN_DV

VMEM_LIMIT_BYTES = 56 * 1024 * 1024
BF16_SUBLANES = 16
MM_MAX_TILE_M = 520
MM_TILE_N = 1024
MM_VMEM_LIMIT_BYTES = 60 * 1024 * 1024
OUT_TILE_M = 512
OUT_TILE_N = 512
SCAN_TILE_L = 512
SCAN_HEADS = 2
ROW_TILE = 128
MEMDEC_ROWS = 4

_NT = (((1,), (1,)), ((), ()))
_TN = (((0,), (0,)), ((), ()))


def _cparams(sem, vmem_limit_bytes=VMEM_LIMIT_BYTES):
    return pltpu.CompilerParams(dimension_semantics=sem, vmem_limit_bytes=vmem_limit_bytes)


def _row_tile(m, limit):
    best = None
    for t in range(BF16_SUBLANES, limit + 1, BF16_SUBLANES):
        if m % t == 0:
            best = t
    assert best is not None, m
    return best


def _sigmoid(x):
    return 0.5 * jnp.tanh(0.5 * x) + 0.5


def _log_sigmoid(x):
    return jnp.minimum(x, 0.0) - jnp.log(1.0 + jnp.exp(-jnp.abs(x)))


def _rope(x, cos, sin):
    half = HEAD_DK // 2
    x1, x2 = x[:, :half], x[:, half:]
    return jnp.concatenate([x1 * cos - x2 * sin, x1 * sin + x2 * cos], axis=-1)


def _head_norm_gate(main, g, z):
    mu = jnp.mean(main, axis=-1, keepdims=True)
    d = main - mu
    var = jnp.mean(d * d, axis=-1, keepdims=True)
    hn = d * lax.rsqrt(var + EPS)
    return hn * g * (z * _sigmoid(z))


def _rms(x, g):
    ms = jnp.mean(x * x, axis=-1, keepdims=True)
    return x * lax.rsqrt(ms + EPS) * g


def _dot(a, b):
    return jnp.dot(a, b, preferred_element_type=F32)


def _rmsnorm_body(x_ref, g_ref, o_ref):
    o_ref[...] = _rms(x_ref[...], g_ref[...]).astype(o_ref.dtype)


def _rmsnorm(x, g, name):
    m, d = x.shape
    tm = ROW_TILE
    return pl.pallas_call(
        _rmsnorm_body,
        grid=(m // tm,),
        in_specs=[pl.BlockSpec((tm, d), lambda i: (i, 0)),
                  pl.BlockSpec((1, d), lambda i: (0, 0))],
        out_specs=pl.BlockSpec((tm, d), lambda i: (i, 0)),
        out_shape=jax.ShapeDtypeStruct((m, d), BF16),
        compiler_params=_cparams(("parallel",)),
        name=name,
    )(x, g.reshape(1, d))


def _rmsnorm_stacked_body(xp_ref, xs_ref, g_ref, o_ref, *, n_prompt_tiles):
    i = pl.program_id(0)

    @pl.when(i < n_prompt_tiles)
    def _():
        o_ref[...] = _rms(xp_ref[...], g_ref[...]).astype(o_ref.dtype)

    @pl.when(i >= n_prompt_tiles)
    def _():
        o_ref[...] = _rms(xs_ref[...], g_ref[...]).astype(o_ref.dtype)


def _rmsnorm_stacked(xp, xs, g):
    mp, d = xp.shape
    ms = xs.shape[0]
    tm = ROW_TILE
    npt, nst = mp // tm, ms // tm
    return pl.pallas_call(
        functools.partial(_rmsnorm_stacked_body, n_prompt_tiles=npt),
        grid=(npt + nst,),
        in_specs=[pl.BlockSpec((tm, d), lambda i: (jnp.minimum(i, npt - 1), 0)),
                  pl.BlockSpec((tm, d), lambda i: (jnp.maximum(i - npt, 0), 0)),
                  pl.BlockSpec((1, d), lambda i: (0, 0))],
        out_specs=pl.BlockSpec((tm, d), lambda i: (i, 0)),
        out_shape=jax.ShapeDtypeStruct((mp + ms, d), BF16),
        compiler_params=_cparams(("arbitrary",)),
        name="pre_norm_0",
    )(xp, xs, g.reshape(1, d))


def _mm_w32_body(a_ref, w_ref, o_ref, wb_ref, *, w_is_nk):
    @pl.when(pl.program_id(1) == 0)
    def _():
        wb_ref[...] = w_ref[...].astype(BF16)

    if w_is_nk:
        acc = lax.dot_general(a_ref[...], wb_ref[...], _NT, preferred_element_type=F32)
    else:
        acc = _dot(a_ref[...], wb_ref[...])
    o_ref[...] = acc.astype(o_ref.dtype)


def _matmul_w32(a, w, layer, col0, ncols, out_dtype, name, w_is_nk=False):
    m, k = a.shape
    tm = _row_tile(m, MM_MAX_TILE_M)
    tn = min(MM_TILE_N, ncols)
    assert col0 % tn == 0 and ncols % tn == 0
    off = col0 // tn
    if w_is_nk:
        w_spec = pl.BlockSpec((None, tn, k), lambda j, i: (layer, off + j, 0))
        wb_shape = (tn, k)
    else:
        w_spec = pl.BlockSpec((None, k, tn), lambda j, i: (layer, 0, off + j))
        wb_shape = (k, tn)
    return pl.pallas_call(
        functools.partial(_mm_w32_body, w_is_nk=w_is_nk),
        grid=(ncols // tn, m // tm),
        in_specs=[pl.BlockSpec((tm, k), lambda j, i: (i, 0)), w_spec],
        out_specs=pl.BlockSpec((tm, tn), lambda j, i: (i, j)),
        out_shape=jax.ShapeDtypeStruct((m, ncols), out_dtype),
        scratch_shapes=[pltpu.VMEM(wb_shape, BF16)],
        compiler_params=_cparams(("parallel", "arbitrary"), MM_VMEM_LIMIT_BYTES),
        name=name,
    )(a, w)


def _mm_body(a_ref, w_ref, o_ref):
    o_ref[...] = _dot(a_ref[...], w_ref[...]).astype(o_ref.dtype)


def _matmul(a, w, out_dtype, name):
    m, k = a.shape
    n = w.shape[1]
    tm = _row_tile(m, MM_MAX_TILE_M)
    return pl.pallas_call(
        _mm_body,
        grid=(m // tm,),
        in_specs=[pl.BlockSpec((tm, k), lambda i: (i, 0)),
                  pl.BlockSpec((k, n), lambda i: (0, 0))],
        out_specs=pl.BlockSpec((tm, n), lambda i: (i, 0)),
        out_shape=jax.ShapeDtypeStruct((m, n), out_dtype),
        compiler_params=_cparams(("parallel",)),
        name=name,
    )(a, w)


def _outproj_body(um_ref, ue_ref, w_ref, o_ref):
    o_ref[...] = (_dot(um_ref[...], w_ref[:MAIN_DV, :]) + _dot(ue_ref[...], w_ref[MAIN_DV:, :]))


def _outproj(u_main, u_mem, w, layer, name):
    m = u_main.shape[0]
    n = w.shape[2]
    tm = _row_tile(m, OUT_TILE_M)
    tn = OUT_TILE_N
    return pl.pallas_call(
        _outproj_body,
        grid=(m // tm, n // tn),
        in_specs=[pl.BlockSpec((tm, MAIN_DV), lambda i, j: (i, 0)),
                  pl.BlockSpec((tm, MEM_DIM), lambda i, j: (i, 0)),
                  pl.BlockSpec((None, MIX_WIDTH, tn), lambda i, j: (layer, 0, j))],
        out_specs=pl.BlockSpec((tm, tn), lambda i, j: (i, j)),
        out_shape=jax.ShapeDtypeStruct((m, n), F32),
        compiler_params=_cparams(("parallel", "parallel")),
        name=name,
    )(u_main, u_mem, w)


def _post_mid_body(op_ref, os_ref, xp_ref, xs_ref, gp_ref, gn_ref, y_ref, h_ref, *, n_prompt_tiles):
    i = pl.program_id(0)

    def emit(o_ref, x_ref):
        y = x_ref[...] + _rms(o_ref[...], gp_ref[...])
        y_ref[...] = y
        h_ref[...] = _rms(y, gn_ref[...]).astype(h_ref.dtype)

    @pl.when(i < n_prompt_tiles)
    def _():
        emit(op_ref, xp_ref)

    @pl.when(i >= n_prompt_tiles)
    def _():
        emit(os_ref, xs_ref)


def _post_mid(out_p, out_s, x_p, x_s, g_post, g_next):
    mp, d = x_p.shape
    ms = x_s.shape[0]
    tm = ROW_TILE
    npt, nst = mp // tm, ms // tm
    prow = pl.BlockSpec((tm, d), lambda i: (jnp.minimum(i, npt - 1), 0))
    srow = pl.BlockSpec((tm, d), lambda i: (jnp.maximum(i - npt, 0), 0))
    vec = pl.BlockSpec((1, d), lambda i: (0, 0))
    orow = pl.BlockSpec((tm, d), lambda i: (i, 0))
    return pl.pallas_call(
        functools.partial(_post_mid_body, n_prompt_tiles=npt),
        grid=(npt + nst,),
        in_specs=[prow, srow, prow, srow, vec, vec],
        out_specs=[orow, orow],
        out_shape=[jax.ShapeDtypeStruct((mp + ms, d), F32),
                   jax.ShapeDtypeStruct((mp + ms, d), BF16)],
        compiler_params=_cparams(("arbitrary",)),
        name="post_0",
    )(out_p, out_s, x_p, x_s, g_post.reshape(1, d), g_next.reshape(1, d))


def _post_last_body(o_ref, x_ref, gp_ref, y_ref):
    y_ref[...] = x_ref[...] + _rms(o_ref[...], gp_ref[...])


def _post_last(out, x_all, row0, g_post, name):
    m, d = out.shape
    tm = ROW_TILE
    off = row0 // tm
    return pl.pallas_call(
        _post_last_body, grid=(m // tm,),
        in_specs=[pl.BlockSpec((tm, d), lambda i: (i, 0)),
                  pl.BlockSpec((tm, d), lambda i: (off + i, 0)),
                  pl.BlockSpec((1, d), lambda i: (0, 0))],
        out_specs=pl.BlockSpec((tm, d), lambda i: (i, 0)),
        out_shape=jax.ShapeDtypeStruct((m, d), F32),
        compiler_params=_cparams(("parallel",)), name=name,
    )(out, x_all, g_post.reshape(1, d))


def _ret_prompt_body(q_ref, k_ref, v_ref, z_ref, cos_ref, sin_ref, din_ref, qd_ref, kd_ref,
                     cd_ref, g_ref, u_ref, s_ref):
    @pl.when(pl.program_id(2) == 0)
    def _():
        s_ref[...] = jnp.zeros_like(s_ref)

    for ci in range(SCAN_TILE_L // CHUNK):
        sl = slice(ci * CHUNK, (ci + 1) * CHUNK)
        cos, sin = cos_ref[sl, :], sin_ref[sl, :]
        for hh in range(SCAN_HEADS):
            ks = slice(hh * HEAD_DK, (hh + 1) * HEAD_DK)
            vs = slice(hh * HEAD_DV, (hh + 1) * HEAD_DV)
            din = din_ref[hh]
            qd = qd_ref[hh]
            kd = kd_ref[hh]
            cd = cd_ref[hh][:, :1]
            q = _rope(q_ref[sl, ks], cos, sin)
            k = _rope(k_ref[sl, ks], cos, sin) * HEAD_DK ** -0.5
            v = v_ref[sl, vs]
            s = s_ref[0, hh]
            sc = lax.dot_general(q.astype(BF16), k.astype(BF16), _NT,
                                 preferred_element_type=F32) * din
            o = _dot(sc.astype(BF16), v) + _dot((q * qd).astype(BF16), s.astype(BF16))
            s_ref[0, hh] = s * cd + lax.dot_general((k * kd).astype(BF16), v, _TN,
                                                    preferred_element_type=F32)
            u_ref[sl, vs] = _head_norm_gate(o, g_ref[:, vs], z_ref[sl, vs]).astype(u_ref.dtype)


def _ret_prompt(qk, v, z, cos, sin, g_head, b, l):
    tl, c, hb = SCAN_TILE_L, CHUNK, SCAN_HEADS
    nt = l // tl
    lg = jnp.log(1.0 - 2.0 ** (-5.0 - jnp.arange(N_HEADS, dtype=F32)))
    idx = jnp.arange(c, dtype=F32)
    diff = idx[:, None] - idx[None, :]
    decay_in = jnp.where(diff[None] >= 0.0,
                         jnp.exp(lg[:, None, None] * jnp.maximum(diff, 0.0)[None]), 0.0)
    q_dec = jnp.exp(lg[:, None] * (idx + 1.0)[None, :])[:, :, None]
    k_dec = jnp.exp(lg[:, None] * (c - 1.0 - idx)[None, :])[:, :, None]
    chunk_dec = jnp.broadcast_to(jnp.exp(lg * c)[:, None, None], (N_HEADS, 1, 128))
    nkb = MAIN_DK // (hb * HEAD_DK)
    return pl.pallas_call(
        _ret_prompt_body,
        grid=(b, N_HEADS // hb, nt),
        in_specs=[
            pl.BlockSpec((tl, hb * HEAD_DK), lambda i, h, t: (i * nt + t, h)),
            pl.BlockSpec((tl, hb * HEAD_DK), lambda i, h, t: (i * nt + t, nkb + h)),
            pl.BlockSpec((tl, hb * HEAD_DV), lambda i, h, t: (i * nt + t, h)),
            pl.BlockSpec((tl, hb * HEAD_DV), lambda i, h, t: (i * nt + t, h)),
            pl.BlockSpec((tl, HEAD_DK // 2), lambda i, h, t: (t, 0)),
            pl.BlockSpec((tl, HEAD_DK // 2), lambda i, h, t: (t, 0)),
            pl.BlockSpec((hb, c, c), lambda i, h, t: (h, 0, 0)),
            pl.BlockSpec((hb, c, 1), lambda i, h, t: (h, 0, 0)),
            pl.BlockSpec((hb, c, 1), lambda i, h, t: (h, 0, 0)),
            pl.BlockSpec((hb, 1, 128), lambda i, h, t: (h, 0, 0)),
            pl.BlockSpec((1, hb * HEAD_DV), lambda i, h, t: (0, h)),
        ],
        out_specs=[
            pl.BlockSpec((tl, hb * HEAD_DV), lambda i, h, t: (i * nt + t, h)),
            pl.BlockSpec((1, hb, HEAD_DK, HEAD_DV), lambda i, h, t: (i, h, 0, 0)),
        ],
        out_shape=[jax.ShapeDtypeStruct((b * l, MAIN_DV), BF16),
                   jax.ShapeDtypeStruct((b, N_HEADS, HEAD_DK, HEAD_DV), F32)],
        compiler_params=_cparams(("parallel", "parallel", "arbitrary")),
        name="ret_prompt",
    )(qk, qk, v, z, cos, sin, decay_in, q_dec, k_dec, chunk_dec, g_head.reshape(1, MAIN_DV))


def _col_from_row(row, eye):
    return jnp.sum(jnp.where(eye, row, 0.0), axis=1, keepdims=True)


def _mlstm_prompt_body(q_ref, k_ref, v_ref, og_ref, z_ref, gi_ref, gf_ref, bi_ref, bf_ref, g_ref,
                       u_ref, c_ref, n_ref, m_ref):
    @pl.when(pl.program_id(2) == 0)
    def _():
        c_ref[...] = jnp.zeros_like(c_ref)
        n_ref[...] = jnp.zeros_like(n_ref)
        m_ref[...] = jnp.full_like(m_ref, NEG_INF)

    c = CHUNK
    ri = lax.broadcasted_iota(jnp.int32, (c, c), 0)
    cj = lax.broadcasted_iota(jnp.int32, (c, c), 1)
    eye = ri == cj
    causal = ri >= cj
    lane8 = lax.broadcasted_iota(jnp.int32, (8, c), 1)
    for ci in range(SCAN_TILE_L // CHUNK):
        sl = slice(ci * CHUNK, (ci + 1) * CHUNK)
        for hh in range(SCAN_HEADS):
            ks = slice(hh * HEAD_DK, (hh + 1) * HEAD_DK)
            vs = slice(hh * HEAD_DV, (hh + 1) * HEAD_DV)
            ic = gi_ref[0, hh, :, sl] + bi_ref[hh]
            lf = _log_sigmoid(gf_ref[0, hh, :, sl] + bf_ref[hh])
            cs = jnp.broadcast_to(lf, (8, c))
            sh = 1
            while sh < c:
                cs = cs + jnp.where(lane8 >= sh, pltpu.roll(cs, sh, 1), 0.0)
                sh *= 2
            b_row = cs[:1, :]
            r_row = ic - b_row
            b_col = _col_from_row(b_row, eye)
            r_col = _col_from_row(r_row, eye)
            b_last = b_row[:, c - 1:]
            m_prev = m_ref[0, hh][:, :1]

            logw = jnp.where(causal, b_col + r_row, NEG_INF)
            log_prev = b_col + m_prev
            m_t = jnp.maximum(log_prev, jnp.max(logw, axis=1, keepdims=True))
            w = jnp.exp(logw - m_t)
            a_prev = jnp.exp(log_prev - m_t)

            q = q_ref[sl, ks]
            k = k_ref[sl, ks] * HEAD_DK ** -0.5
            v = v_ref[sl, vs]
            qb, kb = q.astype(BF16), k.astype(BF16)
            cm = c_ref[0, hh]
            nv = n_ref[0, hh]
            sc = lax.dot_general(qb, kb, _NT, preferred_element_type=F32) * w
            num = _dot(sc.astype(BF16), v) + a_prev * _dot(qb, cm.astype(BF16))
            qn = jnp.sum(qb.astype(F32) * nv.astype(BF16).astype(F32), axis=1, keepdims=True)
            den = jnp.sum(sc, axis=1, keepdims=True) + a_prev * qn
            hc = num * (1.0 / jnp.maximum(jnp.abs(den), jnp.exp(-m_t)))

            m_new = m_t[c - 1:, :]
            a_c = jnp.exp(b_last + m_prev - m_new)
            wk_col = jnp.exp(b_last + r_col - m_new)
            wk_row = jnp.exp(b_last + r_row - m_new)
            c_ref[0, hh] = a_c * cm + lax.dot_general((k * wk_col).astype(BF16), v, _TN,
                                                      preferred_element_type=F32)
            wk8 = jnp.broadcast_to(wk_row, (8, c)).astype(BF16)
            n_ref[0, hh] = a_c * nv + _dot(wk8, kb)[:1, :]
            m_ref[0, hh] = jnp.broadcast_to(m_new, (1, 128))

            main = hc * _sigmoid(og_ref[sl, vs])
            u_ref[sl, vs] = _head_norm_gate(main, g_ref[:, vs], z_ref[sl, vs]).astype(u_ref.dtype)


def _mlstm_prompt(qk, v, og, z, gates_t, bias, g_head, b, l):
    tl, hb = SCAN_TILE_L, SCAN_HEADS
    nt = l // tl
    bias3 = bias.reshape(2 * N_HEADS, 1, 1)
    nkb = MAIN_DK // (hb * HEAD_DK)
    nhb = N_HEADS // hb
    dk_spec = lambda off: pl.BlockSpec((tl, hb * HEAD_DK), lambda i, h, t: (i * nt + t, off + h))
    dv_spec = pl.BlockSpec((tl, hb * HEAD_DV), lambda i, h, t: (i * nt + t, h))
    return pl.pallas_call(
        _mlstm_prompt_body,
        grid=(b, nhb, nt),
        in_specs=[
            dk_spec(0), dk_spec(nkb), dv_spec, dv_spec, dv_spec,
            pl.BlockSpec((1, hb, 1, tl), lambda i, h, t: (i, h, 0, t)),
            pl.BlockSpec((1, hb, 1, tl), lambda i, h, t: (i, nhb + h, 0, t)),
            pl.BlockSpec((hb, 1, 1), lambda i, h, t: (h, 0, 0)),
            pl.BlockSpec((hb, 1, 1), lambda i, h, t: (nhb + h, 0, 0)),
            pl.BlockSpec((1, hb * HEAD_DV), lambda i, h, t: (0, h)),
        ],
        out_specs=[
            dv_spec,
            pl.BlockSpec((1, hb, HEAD_DK, HEAD_DV), lambda i, h, t: (i, h, 0, 0)),
            pl.BlockSpec((1, hb, 1, HEAD_DK), lambda i, h, t: (i, h, 0, 0)),
            pl.BlockSpec((1, hb, 1, 128), lambda i, h, t: (i, h, 0, 0)),
        ],
        out_shape=[jax.ShapeDtypeStruct((b * l, MAIN_DV), BF16),
                   jax.ShapeDtypeStruct((b, N_HEADS, HEAD_DK, HEAD_DV), F32),
                   jax.ShapeDtypeStruct((b, N_HEADS, 1, HEAD_DK), F32),
                   jax.ShapeDtypeStruct((b, N_HEADS, 1, 128), F32)],
        compiler_params=_cparams(("parallel", "parallel", "arbitrary")),
        name="mlstm_prompt",
    )(qk, qk, v, og, z, gates_t, gates_t, bias3, bias3, g_head.reshape(1, MAIN_DV))


def _memattn_prompt_body(q_ref, k_ref, v_ref, z_ref, u_ref):
    q = (q_ref[...] * MEM_HEAD_DIM ** -0.5).astype(BF16)
    s = lax.dot_general(q, k_ref[...].astype(BF16), _NT, preferred_element_type=F32)
    p = jnp.exp(s - jnp.max(s, axis=-1, keepdims=True))
    p = p * (1.0 / jnp.sum(p, axis=-1, keepdims=True))
    o = _dot(p.astype(BF16), v_ref[...].astype(BF16))
    z = z_ref[...]
    u_ref[...] = (o * (z * _sigmoid(z))).astype(u_ref.dtype)


def _memattn_prompt(qm, mk, mv, z, b, l):
    tl = SCAN_TILE_L
    nt = l // tl
    zoff = MAIN_DV // MEM_HEAD_DIM
    return pl.pallas_call(
        _memattn_prompt_body,
        grid=(b, N_MEM_HEADS, nt),
        in_specs=[
            pl.BlockSpec((tl, MEM_HEAD_DIM), lambda i, h, t: (i * nt + t, h)),
            pl.BlockSpec((N_MEM, MEM_HEAD_DIM), lambda i, h, t: (i, h)),
            pl.BlockSpec((N_MEM, MEM_HEAD_DIM), lambda i, h, t: (i, h)),
            pl.BlockSpec((tl, MEM_HEAD_DIM), lambda i, h, t: (i * nt + t, zoff + h)),
        ],
        out_specs=pl.BlockSpec((tl, MEM_HEAD_DIM), lambda i, h, t: (i * nt + t, h)),
        out_shape=jax.ShapeDtypeStruct((b * l, MEM_DIM), BF16),
        compiler_params=_cparams(("parallel", "parallel", "parallel")),
        name="memattn_prompt",
    )(qm, mk, mv, z)


def _col_bcast(row16, width):
    ones = jnp.ones((BF16_SUBLANES, width), BF16)
    return lax.dot_general(row16, ones, _TN, preferred_element_type=F32)


def _first_row16(x):
    n = x.shape[1]
    r = lax.broadcasted_iota(jnp.int32, (BF16_SUBLANES, n), 0)
    return jnp.where(r == 0, jnp.broadcast_to(x, (BF16_SUBLANES, n)), 0.0).astype(BF16)


def _decode_body(a_ref, w_ref, e_ref, qk_ref, v_ref, z_ref, g_ref, s_ref, *rest,
                 use_rope, is_mlstm):
    if use_rope:
        cos_ref, sin_ref = rest[:2]
        rest = rest[2:]
    if is_mlstm:
        og_ref, n_ref, u_ref, so_ref, no_ref = rest
    else:
        u_ref, so_ref = rest
    base = pl.program_id(0) * N_HEADS
    for h in range(N_HEADS):
        a = a_ref[base + h]
        w = w_ref[base + h]
        qs = slice(h * HEAD_DK, (h + 1) * HEAD_DK)
        ks = slice(MAIN_DK + h * HEAD_DK, MAIN_DK + (h + 1) * HEAD_DK)
        vs = slice(h * HEAD_DV, (h + 1) * HEAD_DV)
        q = qk_ref[0, :, qs]
        k = qk_ref[0, :, ks]
        if use_rope:
            q = _rope(q, cos_ref[...], sin_ref[...])
            k = _rope(k, cos_ref[...], sin_ref[...])
        k = k * HEAD_DK ** -0.5
        v = v_ref[0, :, vs]
        vf = v.astype(F32)
        qb = q.astype(BF16)
        kb = k.astype(BF16)
        s = s_ref[0, h]

        qk = jnp.sum(qb.astype(F32) * kb.astype(F32), axis=1, keepdims=True)
        sc = qk * w
        q_cols = _col_bcast(_first_row16(q), HEAD_DV)
        qs_ = jnp.sum(q_cols * s, axis=0, keepdims=True)
        num = sc.astype(BF16).astype(F32) * vf + a * qs_
        kv = lax.dot_general(_first_row16(k * w), jnp.broadcast_to(v, (BF16_SUBLANES, HEAD_DV)),
                             _TN, preferred_element_type=F32)
        so_ref[0, h] = a * s + kv

        if is_mlstm:
            nv = n_ref[0, h]
            qn = jnp.sum(qb.astype(F32) * nv.astype(BF16).astype(F32), axis=1, keepdims=True)
            den = sc + a * qn
            main = num / jnp.maximum(jnp.abs(den), e_ref[base + h])
            wb = (jnp.zeros((1, 1), F32) + w).astype(BF16).astype(F32)
            no_ref[0, h] = a * nv + wb * kb.astype(F32)
            main = main * _sigmoid(og_ref[0, :, vs])
        else:
            main = num
        u_ref[0, :, vs] = _head_norm_gate(main, g_ref[:, vs], z_ref[0, :, vs]).astype(u_ref.dtype)


def _decode(a, w, e, qk, v, z, g_head, state, rope_tabs=None, og=None, n_state=None):
    b = qk.shape[0]
    is_mlstm = og is not None
    use_rope = rope_tabs is not None
    qk_spec = pl.BlockSpec((1, 1, 2 * MAIN_DK), lambda i, *_: (i, 0, 0))
    dv_spec = pl.BlockSpec((1, 1, MAIN_DV), lambda i, *_: (i, 0, 0))
    st_spec = pl.BlockSpec((1, N_HEADS, HEAD_DK, HEAD_DV), lambda i, *_: (i, 0, 0, 0))
    n_spec = pl.BlockSpec((1, N_HEADS, 1, HEAD_DK), lambda i, *_: (i, 0, 0, 0))
    in_specs = [qk_spec, dv_spec, dv_spec,
                pl.BlockSpec((1, MAIN_DV), lambda i, *_: (0, 0)), st_spec]
    args = [qk, v, z, g_head.reshape(1, MAIN_DV), state]
    out_specs = [dv_spec, st_spec]
    out_shape = [jax.ShapeDtypeStruct((b, 1, MAIN_DV), BF16),
                 jax.ShapeDtypeStruct(state.shape, F32)]
    if use_rope:
        tab = pl.BlockSpec((1, HEAD_DK // 2), lambda i, *_: (0, 0))
        in_specs += [tab, tab]
        args += list(rope_tabs)
    if is_mlstm:
        in_specs += [dv_spec, n_spec]
        args += [og, n_state]
        out_specs.append(n_spec)
        out_shape.append(jax.ShapeDtypeStruct(n_state.shape, F32))
    return pl.pallas_call(
        functools.partial(_decode_body, use_rope=use_rope, is_mlstm=is_mlstm),
        grid_spec=pltpu.PrefetchScalarGridSpec(
            num_scalar_prefetch=3, grid=(b,),
            in_specs=in_specs, out_specs=out_specs),
        out_shape=out_shape,
        compiler_params=_cparams(("parallel",)),
        name="mlstm_decode" if is_mlstm else "ret_decode",
    )(a, w, e, *args)


def _gate_decode_body(gi_ref, gf_ref, bi_ref, bf_ref, m_ref, mo_ref, a_ref, w_ref, e_ref):
    ic = gi_ref[...] + bi_ref[...]
    lf = _log_sigmoid(gf_ref[...] + bf_ref[...])
    log_prev = lf + m_ref[...]
    m_t = jnp.maximum(log_prev, ic)
    mo_ref[...] = m_t
    a_ref[...] = jnp.exp(log_prev - m_t)
    w_ref[...] = jnp.exp(ic - m_t)
    e_ref[...] = jnp.exp(-m_t)


def _gate_decode(g_i, g_f, b_i, b_f, m):
    sds = jax.ShapeDtypeStruct(m.shape, F32)
    return pl.pallas_call(
        _gate_decode_body, out_shape=[sds, sds, sds, sds], name="mlstm_decode_gates",
    )(g_i, g_f, b_i.reshape(1, N_HEADS), b_f.reshape(1, N_HEADS), m)


def _lane_tiled(x):
    lead = x.shape[:-2]
    nh, d = x.shape[-2:]
    x = x.reshape(lead + (nh, d // 128, 128))
    x = jnp.swapaxes(x, -3, -2)
    return x.reshape(lead + (d // 128 * nh, 128))


def _memattn_decode_body(q_ref, k_ref, v_ref, z_ref, u_ref):
    rows = MEM_HEAD_DIM // 128 * N_MEM_HEADS
    for bi in range(MEMDEC_ROWS):
        q = q_ref[bi] * MEM_HEAD_DIM ** -0.5
        prod = k_ref[bi] * q[None]
        part = prod[:, :rows // 2, :] + prod[:, rows // 2:, :]
        part = part + pltpu.roll(part, N_MEM_HEADS, 1)
        s = jnp.sum(part, axis=-1, keepdims=True)
        p = jnp.exp(s - jnp.max(s, axis=0, keepdims=True))
        p = p / jnp.sum(p, axis=0, keepdims=True)
        o = jnp.concatenate([jnp.sum(p * v_ref[bi, :, :rows // 2, :], axis=0),
                             jnp.sum(p * v_ref[bi, :, rows // 2:, :], axis=0)], axis=0)
        z = z_ref[bi]
        u_ref[bi] = (o * (z * _sigmoid(z))).astype(u_ref.dtype)


def _memattn_decode(qm, cache_k, cache_v, z, layer):
    b = qm.shape[0]
    nb = MEMDEC_ROWS
    rows = qm.shape[1]
    row = pl.BlockSpec((nb, rows, 128), lambda i: (i, 0, 0))
    kv = pl.BlockSpec((None, nb, N_MEM, rows, 128), lambda i: (layer, i, 0, 0, 0))
    return pl.pallas_call(
        _memattn_decode_body,
        grid=(b // nb,),
        in_specs=[row, kv, kv, row],
        out_specs=row,
        out_shape=jax.ShapeDtypeStruct((b, rows, 128), BF16),
        compiler_params=_cparams(("parallel",)),
        name=f"memattn_decode_{layer}",
    )(qm, cache_k, cache_v, z)


def _from_lane_tiled(x, nh):
    lead = x.shape[:-2]
    nt = x.shape[-2] // nh
    x = x.reshape(lead + (nt, nh, 128))
    x = jnp.swapaxes(x, -3, -2)
    return x.reshape(lead + (nh, nt * 128))


def _rope_tables(pos):
    d = HEAD_DK
    inv = 1.0 / (ROPE_THETA ** (jnp.arange(0, d, 2, dtype=F32) / d))
    ang = pos[:, None] * inv[None, :]
    return jnp.cos(ang), jnp.sin(ang)


_REGIONS = (("qk", COL_QK, COL_V, F32), ("v", COL_V, COL_QM, BF16), ("qm", COL_QM, COL_Z, F32),
            ("z", COL_Z, COL_OG, F32), ("og", COL_OG, COL_GATES, F32))


def _in_proj(h, w, tag, with_mlstm):
    w_is_nk = with_mlstm
    wk = jnp.swapaxes(w, 1, 2) if w_is_nk else w
    out = {}
    for name, c0, c1, dt in _REGIONS:
        if name == "og" and not with_mlstm:
            continue
        out[name] = _matmul_w32(h, wk, 0, c0, c1 - c0, dt, f"inproj_{name}_{tag}", w_is_nk)
    if with_mlstm:
        wg = jnp.pad(w[0, :, COL_GATES:], ((0, 0), (0, 128 - 2 * N_HEADS))).astype(BF16)
        out["gates"] = _matmul(h, wg, F32, f"inproj_gates_{tag}")
    return out


def kernel(x_prompt, x_sample, mem_prompt, state_ret, state_mlstm_C, state_mlstm_n, state_mlstm_m,
           cache_mem_k, cache_mem_v, g_pre, g_post, w_in_ret, w_in_mlstm, b_gate_mlstm, g_head,
           w_out, g_mem, w_mem_kv):
    bp, lp, _ = x_prompt.shape
    bs = x_sample.shape[0]
    mp = bp * lp

    w_o = w_out.astype(BF16)

    mem2d = mem_prompt.reshape(bp * N_MEM, D_MODEL)
    mem_k, mem_v = [], []
    for l in range(DEPTH):
        hm = _rmsnorm(mem2d, g_mem[l], f"mem_norm_{l}")
        mem_k.append(_matmul_w32(hm, w_mem_kv, l, 0, MEM_DIM, F32, f"mem_k_{l}"))
        mem_v.append(_matmul_w32(hm, w_mem_kv, l, MEM_DIM, MEM_DIM, F32, f"mem_v_{l}"))

    cos_p, sin_p = _rope_tables(jnp.arange(lp, dtype=F32))
    cos_s, sin_s = _rope_tables(PAST_LEN + jnp.arange(1, dtype=F32))
    bias = b_gate_mlstm[0]
    xp = x_prompt.reshape(mp, D_MODEL)
    xs = x_sample.reshape(bs, D_MODEL)
    srow = lambda a: a[mp:].reshape(bs, 1, a.shape[-1])
    smem = lambda a: _lane_tiled(a[mp:].reshape(bs, N_MEM_HEADS, MEM_HEAD_DIM))
    cache_k = _lane_tiled(cache_mem_k)
    cache_v = _lane_tiled(cache_mem_v)
    unmem = lambda u: _from_lane_tiled(u, N_MEM_HEADS).reshape(bs, MEM_DIM)

    h = _rmsnorm_stacked(xp, xs, g_pre[0])
    pr = _in_proj(h, w_in_ret, "0", False)
    u_main_p, ret_prompt = _ret_prompt(pr["qk"], pr["v"], pr["z"], cos_p, sin_p, g_head[0], bp, lp)
    u_mem_p = _memattn_prompt(pr["qm"], mem_k[0], mem_v[0], pr["z"], bp, lp)
    out_p = _outproj(u_main_p, u_mem_p, w_o, 0, "outproj_p0")

    gamma = 1.0 - 2.0 ** (-5.0 - jnp.arange(N_HEADS, dtype=F32))
    a_ret = jnp.exp(jnp.log(gamma))
    a_ret = jnp.broadcast_to(a_ret[None, :], (bs, N_HEADS)).reshape(-1)
    ones = jnp.ones((bs * N_HEADS,), F32)
    u_main_s, ret_sample = _decode(a_ret, ones, ones, srow(pr["qk"]), srow(pr["v"]), srow(pr["z"]),
                                   g_head[0], state_ret[0], rope_tabs=(cos_s, sin_s))
    u_mem_s = _memattn_decode(smem(pr["qm"]), cache_k, cache_v, smem(pr["z"][:, MAIN_DV:]), 0)
    out_s = _outproj(u_main_s.reshape(bs, MAIN_DV), unmem(u_mem_s), w_o, 0,
                     "outproj_s0")
    x1, h = _post_mid(out_p, out_s, xp, xs, g_post[0], g_pre[1])

    pr = _in_proj(h, w_in_mlstm, "1", True)
    gates = pr["gates"][:, :2 * N_HEADS]
    gates_t = gates[:mp].reshape(bp, lp, 2 * N_HEADS).transpose(0, 2, 1)[:, :, None, :]
    u_main_p, c_prompt, n_prompt, m_prompt = _mlstm_prompt(
        pr["qk"], pr["v"], pr["og"], pr["z"], gates_t, bias, g_head[1], bp, lp)
    u_mem_p = _memattn_prompt(pr["qm"], mem_k[1], mem_v[1], pr["z"], bp, lp)
    out_p = _outproj(u_main_p, u_mem_p, w_o, 1, "outproj_p1")
    y_prompt = _post_last(out_p, x1, 0, g_post[1], "post_p1")

    g_i = gates[mp:, :N_HEADS]
    g_f = gates[mp:, N_HEADS:]
    m_new, a_m, w_m, e_m = _gate_decode(g_i, g_f, bias[:N_HEADS], bias[N_HEADS:], state_mlstm_m[0])
    u_main_s, c_sample, n_sample = _decode(
        a_m.reshape(-1), w_m.reshape(-1), e_m.reshape(-1), srow(pr["qk"]), srow(pr["v"]),
        srow(pr["z"]), g_head[1], state_mlstm_C[0], og=srow(pr["og"]),
        n_state=state_mlstm_n[0].reshape(bs, N_HEADS, 1, HEAD_DK))
    u_mem_s = _memattn_decode(smem(pr["qm"]), cache_k, cache_v, smem(pr["z"][:, MAIN_DV:]), 1)
    out_s = _outproj(u_main_s.reshape(bs, MAIN_DV), unmem(u_mem_s), w_o, 1,
                     "outproj_s1")
    y_sample = _post_last(out_s, x1, mp, g_post[1], "post_s1")

    shape_kv = (DEPTH, bp, N_MEM, N_MEM_HEADS, MEM_HEAD_DIM)
    return (
        y_prompt.reshape(bp, lp, D_MODEL),
        y_sample.reshape(bs, 1, D_MODEL),
        ret_prompt[None],
        c_prompt[None],
        n_prompt.reshape(1, bp, N_HEADS, HEAD_DK),
        m_prompt[:, :, 0, 0][None],
        jnp.stack(mem_k).reshape(shape_kv),
        jnp.stack(mem_v).reshape(shape_kv),
        ret_sample[None],
        c_sample[None],
        n_sample.reshape(1, bs, N_HEADS, HEAD_DK),
        m_new[None],
    )
```

```python
import functools

import jax
import jax.numpy as jnp
from jax import lax
from jax.experimental import pallas as pl
from jax.experimental.pallas import tpu as pltpu

F32 = jnp.float32
BF16 = jnp.bfloat16

D_MODEL = 4096
DEPTH = 2
PAST_LEN = 16384
MIX_WIDTH = 2 * D_MODEL
N_HEADS = 12
HEAD_DV = MIX_WIDTH // 16
HEAD_DK = HEAD_DV // 2
MAIN_DK = N_HEADS * HEAD_DK
MAIN_DV = N_HEADS * HEAD_DV
N_MEM = 256
N_MEM_HEADS = 4
MEM_HEAD_DIM = MIX_WIDTH // 16
MEM_DIM = N_MEM_HEADS * MEM_HEAD_DIM
CHUNK = 128
ROPE_THETA = 10000.0
EPS = 1e-6
NEG_INF = -1e30

COL_QK = 0
COL_V = 2 * MAIN_DK
COL_QM = COL_V + MAIN_DV
COL_Z = COL_QM + MEM_DIM
COL_OG = COL_Z + MIX_WIDTH
COL_GATES = COL_OG + MAIN_DV

VMEM_LIMIT_BYTES = 56 * 1024 * 1024
BF16_SUBLANES = 16
MM_MAX_TILE_M = 1040
MM_TILE_N = 1024
MM_WEIGHT_CHUNKS = 8
OUT_TILE_M = 512
OUT_TILE_N = 512
OUT_WEIGHT_CHUNKS = 16
MEM_TILE_M = 512
MEM_TILE_N = 512
SCAN_TILE_L = 512
SCAN_HEADS = 2
ROW_TILE = 128
MEMDEC_ROWS = 4

_NT = (((1,), (1,)), ((), ()))
_TN = (((0,), (0,)), ((), ()))


def _cparams(sem, vmem_limit_bytes=VMEM_LIMIT_BYTES):
    return pltpu.CompilerParams(dimension_semantics=sem, vmem_limit_bytes=vmem_limit_bytes)


def _row_tile(m, limit):
    best = None
    for t in range(BF16_SUBLANES, limit + 1, BF16_SUBLANES):
        if m % t == 0:
            best = t
    assert best is not None, m
    return best


def _sigmoid(x):
    return 0.5 * jnp.tanh(0.5 * x) + 0.5


def _log_sigmoid(x):
    return jnp.minimum(x, 0.0) - jnp.log(1.0 + jnp.exp(-jnp.abs(x)))


def _rope(x, cos, sin):
    half = HEAD_DK // 2
    x1, x2 = x[:, :half], x[:, half:]
    return jnp.concatenate([x1 * cos - x2 * sin, x1 * sin + x2 * cos], axis=-1)


def _head_norm_gate(main, g, z):
    mu = jnp.mean(main, axis=-1, keepdims=True)
    d = main - mu
    var = jnp.mean(d * d, axis=-1, keepdims=True)
    hn = d * lax.rsqrt(var + EPS)
    return hn * g * (z * _sigmoid(z))


def _rms(x, g):
    ms = jnp.mean(x * x, axis=-1, keepdims=True)
    return x * lax.rsqrt(ms + EPS) * g


def _dot(a, b):
    return jnp.dot(a, b, preferred_element_type=F32)


def _rmsnorm_body(x_ref, g_ref, o_ref):
    o_ref[...] = _rms(x_ref[...], g_ref[...]).astype(o_ref.dtype)


def _rmsnorm(x, g, name):
    m, d = x.shape
    tm = ROW_TILE
    return pl.pallas_call(
        _rmsnorm_body,
        grid=(m // tm,),
        in_specs=[pl.BlockSpec((tm, d), lambda i: (i, 0)),
                  pl.BlockSpec((1, d), lambda i: (0, 0))],
        out_specs=pl.BlockSpec((tm, d), lambda i: (i, 0)),
        out_shape=jax.ShapeDtypeStruct((m, d), BF16),
        compiler_params=_cparams(("parallel",)),
        name=name,
    )(x, g.reshape(1, d))


def _rmsnorm_stacked_body(xp_ref, xs_ref, g_ref, o_ref, *, n_prompt_tiles):
    i = pl.program_id(0)

    @pl.when(i < n_prompt_tiles)
    def _():
        o_ref[...] = _rms(xp_ref[...], g_ref[...]).astype(o_ref.dtype)

    @pl.when(i >= n_prompt_tiles)
    def _():
        o_ref[...] = _rms(xs_ref[...], g_ref[...]).astype(o_ref.dtype)


def _rmsnorm_stacked(xp, xs, g):
    mp, d = xp.shape
    ms = xs.shape[0]
    tm = ROW_TILE
    npt, nst = mp // tm, ms // tm
    return pl.pallas_call(
        functools.partial(_rmsnorm_stacked_body, n_prompt_tiles=npt),
        grid=(npt + nst,),
        in_specs=[pl.BlockSpec((tm, d), lambda i: (jnp.minimum(i, npt - 1), 0)),
                  pl.BlockSpec((tm, d), lambda i: (jnp.maximum(i - npt, 0), 0)),
                  pl.BlockSpec((1, d), lambda i: (0, 0))],
        out_specs=pl.BlockSpec((tm, d), lambda i: (i, 0)),
        out_shape=jax.ShapeDtypeStruct((mp + ms, d), BF16),
        compiler_params=_cparams(("arbitrary",)),
        name="pre_norm_0",
    )(xp, xs, g.reshape(1, d))


def _mm_w32_body(a_ref, w_ref, o_ref, wb_ref, *, w_is_nk):
    @pl.when(pl.program_id(1) == 0)
    def _():
        wb_ref[...] = w_ref[...].astype(BF16)

    if w_is_nk:
        acc = lax.dot_general(a_ref[...], wb_ref[...], _NT, preferred_element_type=F32)
    else:
        acc = _dot(a_ref[...], wb_ref[...])
    o_ref[...] = acc.astype(o_ref.dtype)


def _matmul_w32(a, w, layer, col0, ncols, out_dtype, name, w_is_nk=False):
    m, k = a.shape
    tm = _row_tile(m, MEM_TILE_M)
    tn = min(MEM_TILE_N, ncols)
    assert col0 % tn == 0 and ncols % tn == 0
    off = col0 // tn
    if w_is_nk:
        w_spec = pl.BlockSpec((None, tn, k), lambda j, i: (layer, off + j, 0))
        wb_shape = (tn, k)
    else:
        w_spec = pl.BlockSpec((None, k, tn), lambda j, i: (layer, 0, off + j))
        wb_shape = (k, tn)
    return pl.pallas_call(
        functools.partial(_mm_w32_body, w_is_nk=w_is_nk),
        grid=(ncols // tn, m // tm),
        in_specs=[pl.BlockSpec((tm, k), lambda j, i: (i, 0)), w_spec],
        out_specs=pl.BlockSpec((tm, tn), lambda j, i: (i, j)),
        out_shape=jax.ShapeDtypeStruct((m, ncols), out_dtype),
        scratch_shapes=[pltpu.VMEM(wb_shape, BF16)],
        compiler_params=_cparams(("parallel", "arbitrary")),
        name=name,
    )(a, w)


def _mm_body(a_ref, w_ref, o_ref):
    o_ref[...] = _dot(a_ref[...], w_ref[...]).astype(o_ref.dtype)


def _matmul(a, w, out_dtype, name):
    m, k = a.shape
    n = w.shape[1]
    tm = _row_tile(m, MM_MAX_TILE_M)
    return pl.pallas_call(
        _mm_body,
        grid=(m // tm,),
        in_specs=[pl.BlockSpec((tm, k), lambda i: (i, 0)),
                  pl.BlockSpec((k, n), lambda i: (0, 0))],
        out_specs=pl.BlockSpec((tm, n), lambda i: (i, 0)),
        out_shape=jax.ShapeDtypeStruct((m, n), out_dtype),
        compiler_params=_cparams(("parallel",)),
        name=name,
    )(a, w)


class _WeightStream:
    def __init__(self, w_hbm, wb_ref, stage_ref, sem, *, layer, col0, tn, n_chunks, w_is_nk):
        self.w_hbm, self.wb_ref, self.stage_ref, self.sem = w_hbm, wb_ref, stage_ref, sem
        self.layer, self.col0, self.tn, self.w_is_nk = layer, col0, tn, w_is_nk
        self.n_chunks = n_chunks
        self.cr = wb_ref.shape[1] // n_chunks

    def copy(self, tile, c, buf):
        if self.w_is_nk:
            row0 = pl.multiple_of(self.col0 + tile * self.tn + c * self.cr, self.cr)
            src = self.w_hbm.at[self.layer, pl.ds(row0, self.cr), :]
        else:
            row0 = pl.multiple_of(c * self.cr, self.cr)
            col0 = pl.multiple_of(self.col0 + tile * self.tn, self.tn)
            src = self.w_hbm.at[self.layer, pl.ds(row0, self.cr), pl.ds(col0, self.tn)]
        return pltpu.make_async_copy(src, self.stage_ref.at[buf], self.sem.at[buf])

    def fill_first_tile(self):
        self.copy(0, 0, 0).start()
        for c in range(self.n_chunks):
            if c + 1 < self.n_chunks:
                self.copy(0, c + 1, (c + 1) % 2).start()
            self.copy(0, c, c % 2).wait()
            self.wb_ref[0, c * self.cr:(c + 1) * self.cr, :] = self.stage_ref[c % 2].astype(BF16)

    def cast_into(self, slot, c):
        rows = pl.ds(pl.multiple_of(c * self.cr, self.cr), self.cr)
        self.wb_ref[slot, rows, :] = self.stage_ref[0].astype(BF16)


def _stream_step(stream, j, i, n_col_tiles, products):
    slot = lax.rem(j, 2)
    nxt = jnp.minimum(j + 1, n_col_tiles - 1)
    c = jnp.minimum(i, stream.n_chunks - 1)
    cp = stream.copy(nxt, c, 0)
    cp.start()
    products[0](slot)
    cp.wait()
    stream.cast_into(1 - slot, c)
    products[1](slot)


def _w_half(wb_ref, slot, half, w_is_nk):
    w = wb_ref.at[slot]
    if w_is_nk:
        hn = wb_ref.shape[1] // 2
        return w[half * hn:(half + 1) * hn, :]
    hn = wb_ref.shape[2] // 2
    return w[:, half * hn:(half + 1) * hn]


def _mm(a, w, w_is_nk):
    if w_is_nk:
        return lax.dot_general(a, w, _NT, preferred_element_type=F32)
    return _dot(a, w)


def _inproj_stream_body(a_ref, w_hbm, o_ref, wb_ref, stage_ref, sem, *, layer, col0, n_chunks,
                        w_is_nk):
    j, i = pl.program_id(0), pl.program_id(1)
    tn = o_ref.shape[1]
    stream = _WeightStream(w_hbm, wb_ref, stage_ref, sem, layer=layer, col0=col0, tn=tn,
                           n_chunks=n_chunks, w_is_nk=w_is_nk)

    @pl.when((j == 0) & (i == 0))
    def _():
        stream.fill_first_tile()

    def half(hf):
        def run(slot):
            acc = _mm(a_ref[...], _w_half(wb_ref, slot, hf, w_is_nk), w_is_nk)
            o_ref[:, hf * (tn // 2):(hf + 1) * (tn // 2)] = acc.astype(o_ref.dtype)
        return run

    _stream_step(stream, j, i, pl.num_programs(0), [half(0), half(1)])


def _inproj_stream(a, w, layer, col0, ncols, out_dtype, name, w_is_nk):
    m, k = a.shape
    tm = _row_tile(m, MM_MAX_TILE_M)
    tn = MM_TILE_N
    n_chunks = MM_WEIGHT_CHUNKS
    assert ncols % tn == 0 and m // tm >= n_chunks
    wshape = (tn, k) if w_is_nk else (k, tn)
    return pl.pallas_call(
        functools.partial(_inproj_stream_body, layer=layer, col0=col0, n_chunks=n_chunks,
                          w_is_nk=w_is_nk),
        grid=(ncols // tn, m // tm),
        in_specs=[pl.BlockSpec((tm, k), lambda j, i: (i, 0)),
                  pl.BlockSpec(memory_space=pl.ANY)],
        out_specs=pl.BlockSpec((tm, tn), lambda j, i: (i, j)),
        out_shape=jax.ShapeDtypeStruct((m, ncols), out_dtype),
        scratch_shapes=[pltpu.VMEM((2,) + wshape, BF16),
                        pltpu.VMEM((2, wshape[0] // n_chunks, wshape[1]), F32),
                        pltpu.SemaphoreType.DMA((2,))],
        compiler_params=_cparams(("arbitrary", "arbitrary")),
        name=name,
    )(a, w)


def _outproj_stream_body(ump_ref, uep_ref, ums_ref, ues_ref, w_hbm, op_ref, os_ref,
                         wb_ref, stage_ref, sem, *, layer, n_chunks, n_prompt_tiles):
    j, i = pl.program_id(0), pl.program_id(1)
    tn = op_ref.shape[1]
    stream = _WeightStream(w_hbm, wb_ref, stage_ref, sem, layer=layer, col0=0, tn=tn,
                           n_chunks=n_chunks, w_is_nk=False)

    @pl.when((j == 0) & (i == 0))
    def _():
        stream.fill_first_tile()

    def halves(um_ref, ue_ref, o_ref):
        def main_part(slot):
            o_ref[...] = _dot(um_ref[...], wb_ref[slot, :MAIN_DV, :])

        def mem_part(slot):
            o_ref[...] += _dot(ue_ref[...], wb_ref[slot, MAIN_DV:, :])

        return [main_part, mem_part]

    @pl.when(i < n_prompt_tiles)
    def _():
        _stream_step(stream, j, i, pl.num_programs(0), halves(ump_ref, uep_ref, op_ref))

    @pl.when(i >= n_prompt_tiles)
    def _():
        _stream_step(stream, j, i, pl.num_programs(0), halves(ums_ref, ues_ref, os_ref))


def _outproj_stream(um_p, ue_p, um_s, ue_s, w, layer):
    mp, ms = um_p.shape[0], um_s.shape[0]
    n = w.shape[2]
    tm = _row_tile(mp, OUT_TILE_M)
    tn = OUT_TILE_N
    npt = mp // tm
    n_chunks = OUT_WEIGHT_CHUNKS
    assert npt + 1 >= n_chunks
    prow = lambda width: pl.BlockSpec((tm, width), lambda j, i: (jnp.minimum(i, npt - 1), 0))
    srow = lambda width: pl.BlockSpec((ms, width), lambda j, i: (0, 0))
    return pl.pallas_call(
        functools.partial(_outproj_stream_body, layer=layer, n_chunks=n_chunks,
                          n_prompt_tiles=npt),
        grid=(n // tn, npt + 1),
        in_specs=[prow(MAIN_DV), prow(MEM_DIM), srow(MAIN_DV), srow(MEM_DIM),
                  pl.BlockSpec(memory_space=pl.ANY)],
        out_specs=[pl.BlockSpec((tm, tn), lambda j, i: (jnp.minimum(i, npt - 1), j)),
                   pl.BlockSpec((ms, tn), lambda j, i: (0, j))],
        out_shape=[jax.ShapeDtypeStruct((mp, n), F32), jax.ShapeDtypeStruct((ms, n), F32)],
        scratch_shapes=[pltpu.VMEM((2, MIX_WIDTH, tn), BF16),
                        pltpu.VMEM((2, MIX_WIDTH // n_chunks, tn), F32),
                        pltpu.SemaphoreType.DMA((2,))],
        compiler_params=_cparams(("arbitrary", "arbitrary")),
        name=f"outproj_{layer}",
    )(um_p, ue_p, um_s, ue_s, w)


def _post_mid_body(op_ref, os_ref, xp_ref, xs_ref, gp_ref, gn_ref, y_ref, h_ref, *, n_prompt_tiles):
    i = pl.program_id(0)

    def emit(o_ref, x_ref):
        y = x_ref[...] + _rms(o_ref[...], gp_ref[...])
        y_ref[...] = y
        h_ref[...] = _rms(y, gn_ref[...]).astype(h_ref.dtype)

    @pl.when(i < n_prompt_tiles)
    def _():
        emit(op_ref, xp_ref)

    @pl.when(i >= n_prompt_tiles)
    def _():
        emit(os_ref, xs_ref)


def _post_mid(out_p, out_s, x_p, x_s, g_post, g_next):
    mp, d = x_p.shape
    ms = x_s.shape[0]
    tm = ROW_TILE
    npt, nst = mp // tm, ms // tm
    prow = pl.BlockSpec((tm, d), lambda i: (jnp.minimum(i, npt - 1), 0))
    srow = pl.BlockSpec((tm, d), lambda i: (jnp.maximum(i - npt, 0), 0))
    vec = pl.BlockSpec((1, d), lambda i: (0, 0))
    orow = pl.BlockSpec((tm, d), lambda i: (i, 0))
    return pl.pallas_call(
        functools.partial(_post_mid_body, n_prompt_tiles=npt),
        grid=(npt + nst,),
        in_specs=[prow, srow, prow, srow, vec, vec],
        out_specs=[orow, orow],
        out_shape=[jax.ShapeDtypeStruct((mp + ms, d), F32),
                   jax.ShapeDtypeStruct((mp + ms, d), BF16)],
        compiler_params=_cparams(("arbitrary",)),
        name="post_0",
    )(out_p, out_s, x_p, x_s, g_post.reshape(1, d), g_next.reshape(1, d))


def _post_last_body(o_ref, x_ref, gp_ref, y_ref):
    y_ref[...] = x_ref[...] + _rms(o_ref[...], gp_ref[...])


def _post_last(out, x_all, row0, g_post, name):
    m, d = out.shape
    tm = ROW_TILE
    off = row0 // tm
    return pl.pallas_call(
        _post_last_body, grid=(m // tm,),
        in_specs=[pl.BlockSpec((tm, d), lambda i: (i, 0)),
                  pl.BlockSpec((tm, d), lambda i: (off + i, 0)),
                  pl.BlockSpec((1, d), lambda i: (0, 0))],
        out_specs=pl.BlockSpec((tm, d), lambda i: (i, 0)),
        out_shape=jax.ShapeDtypeStruct((m, d), F32),
        compiler_params=_cparams(("parallel",)), name=name,
    )(out, x_all, g_post.reshape(1, d))


def _ret_prompt_body(q_ref, k_ref, v_ref, z_ref, cos_ref, sin_ref, din_ref, qd_ref, kd_ref,
                     cd_ref, g_ref, u_ref, s_ref):
    @pl.when(pl.program_id(2) == 0)
    def _():
        s_ref[...] = jnp.zeros_like(s_ref)

    for ci in range(SCAN_TILE_L // CHUNK):
        sl = slice(ci * CHUNK, (ci + 1) * CHUNK)
        cos, sin = cos_ref[sl, :], sin_ref[sl, :]
        for hh in range(SCAN_HEADS):
            ks = slice(hh * HEAD_DK, (hh + 1) * HEAD_DK)
            vs = slice(hh * HEAD_DV, (hh + 1) * HEAD_DV)
            din = din_ref[hh]
            qd = qd_ref[hh]
            kd = kd_ref[hh]
            cd = cd_ref[hh][:, :1]
            q = _rope(q_ref[sl, ks], cos, sin)
            k = _rope(k_ref[sl, ks], cos, sin) * HEAD_DK ** -0.5
            v = v_ref[sl, vs]
            s = s_ref[0, hh]
            sc = lax.dot_general(q.astype(BF16), k.astype(BF16), _NT,
                                 preferred_element_type=F32) * din
            o = _dot(sc.astype(BF16), v) + _dot((q * qd).astype(BF16), s.astype(BF16))
            s_ref[0, hh] = s * cd + lax.dot_general((k * kd).astype(BF16), v, _TN,
                                                    preferred_element_type=F32)
            u_ref[sl, vs] = _head_norm_gate(o, g_ref[:, vs], z_ref[sl, vs]).astype(u_ref.dtype)


def _ret_prompt(qk, v, z, cos, sin, g_head, b, l):
    tl, c, hb = SCAN_TILE_L, CHUNK, SCAN_HEADS
    nt = l // tl
    lg = jnp.log(1.0 - 2.0 ** (-5.0 - jnp.arange(N_HEADS, dtype=F32)))
    idx = jnp.arange(c, dtype=F32)
    diff = idx[:, None] - idx[None, :]
    decay_in = jnp.where(diff[None] >= 0.0,
                         jnp.exp(lg[:, None, None] * jnp.maximum(diff, 0.0)[None]), 0.0)
    q_dec = jnp.exp(lg[:, None] * (idx + 1.0)[None, :])[:, :, None]
    k_dec = jnp.exp(lg[:, None] * (c - 1.0 - idx)[None, :])[:, :, None]
    chunk_dec = jnp.broadcast_to(jnp.exp(lg * c)[:, None, None], (N_HEADS, 1, 128))
    nkb = MAIN_DK // (hb * HEAD_DK)
    return pl.pallas_call(
        _ret_prompt_body,
        grid=(b, N_HEADS // hb, nt),
        in_specs=[
            pl.BlockSpec((tl, hb * HEAD_DK), lambda i, h, t: (i * nt + t, h)),
            pl.BlockSpec((tl, hb * HEAD_DK), lambda i, h, t: (i * nt + t, nkb + h)),
            pl.BlockSpec((tl, hb * HEAD_DV), lambda i, h, t: (i * nt + t, h)),
            pl.BlockSpec((tl, hb * HEAD_DV), lambda i, h, t: (i * nt + t, h)),
            pl.BlockSpec((tl, HEAD_DK // 2), lambda i, h, t: (t, 0)),
            pl.BlockSpec((tl, HEAD_DK // 2), lambda i, h, t: (t, 0)),
            pl.BlockSpec((hb, c, c), lambda i, h, t: (h, 0, 0)),
            pl.BlockSpec((hb, c, 1), lambda i, h, t: (h, 0, 0)),
            pl.BlockSpec((hb, c, 1), lambda i, h, t: (h, 0, 0)),
            pl.BlockSpec((hb, 1, 128), lambda i, h, t: (h, 0, 0)),
            pl.BlockSpec((1, hb * HEAD_DV), lambda i, h, t: (0, h)),
        ],
        out_specs=[
            pl.BlockSpec((tl, hb * HEAD_DV), lambda i, h, t: (i * nt + t, h)),
            pl.BlockSpec((1, hb, HEAD_DK, HEAD_DV), lambda i, h, t: (i, h, 0, 0)),
        ],
        out_shape=[jax.ShapeDtypeStruct((b * l, MAIN_DV), BF16),
                   jax.ShapeDtypeStruct((b, N_HEADS, HEAD_DK, HEAD_DV), F32)],
        compiler_params=_cparams(("parallel", "parallel", "arbitrary")),
        name="ret_prompt",
    )(qk, qk, v, z, cos, sin, decay_in, q_dec, k_dec, chunk_dec, g_head.reshape(1, MAIN_DV))


def _col_from_row(row, eye):
    return jnp.sum(jnp.where(eye, row, 0.0), axis=1, keepdims=True)


def _mlstm_prompt_body(q_ref, k_ref, v_ref, og_ref, z_ref, gi_ref, gf_ref, bi_ref, bf_ref, g_ref,
                       u_ref, c_ref, n_ref, m_ref):
    @pl.when(pl.program_id(2) == 0)
    def _():
        c_ref[...] = jnp.zeros_like(c_ref)
        n_ref[...] = jnp.zeros_like(n_ref)
        m_ref[...] = jnp.full_like(m_ref, NEG_INF)

    c = CHUNK
    ri = lax.broadcasted_iota(jnp.int32, (c, c), 0)
    cj = lax.broadcasted_iota(jnp.int32, (c, c), 1)
    eye = ri == cj
    causal = ri >= cj
    lane8 = lax.broadcasted_iota(jnp.int32, (8, c), 1)
    for ci in range(SCAN_TILE_L // CHUNK):
        sl = slice(ci * CHUNK, (ci + 1) * CHUNK)
        for hh in range(SCAN_HEADS):
            ks = slice(hh * HEAD_DK, (hh + 1) * HEAD_DK)
            vs = slice(hh * HEAD_DV, (hh + 1) * HEAD_DV)
            ic = gi_ref[0, hh, :, sl] + bi_ref[hh]
            lf = _log_sigmoid(gf_ref[0, hh, :, sl] + bf_ref[hh])
            cs = jnp.broadcast_to(lf, (8, c))
            sh = 1
            while sh < c:
                cs = cs + jnp.where(lane8 >= sh, pltpu.roll(cs, sh, 1), 0.0)
                sh *= 2
            b_row = cs[:1, :]
            r_row = ic - b_row
            b_col = _col_from_row(b_row, eye)
            r_col = _col_from_row(r_row, eye)
            b_last = b_row[:, c - 1:]
            m_prev = m_ref[0, hh][:, :1]

            logw = jnp.where(causal, b_col + r_row, NEG_INF)
            log_prev = b_col + m_prev
            m_t = jnp.maximum(log_prev, jnp.max(logw, axis=1, keepdims=True))
            w = jnp.exp(logw - m_t)
            a_prev = jnp.exp(log_prev - m_t)

            q = q_ref[sl, ks]
            k = k_ref[sl, ks] * HEAD_DK ** -0.5
            v = v_ref[sl, vs]
            qb, kb = q.astype(BF16), k.astype(BF16)
            cm = c_ref[0, hh]
            nv = n_ref[0, hh]
            sc = lax.dot_general(qb, kb, _NT, preferred_element_type=F32) * w
            num = _dot(sc.astype(BF16), v) + a_prev * _dot(qb, cm.astype(BF16))
            qn = jnp.sum(qb.astype(F32) * nv.astype(BF16).astype(F32), axis=1, keepdims=True)
            den = jnp.sum(sc, axis=1, keepdims=True) + a_prev * qn
            hc = num * (1.0 / jnp.maximum(jnp.abs(den), jnp.exp(-m_t)))

            m_new = m_t[c - 1:, :]
            a_c = jnp.exp(b_last + m_prev - m_new)
            wk_col = jnp.exp(b_last + r_col - m_new)
            wk_row = jnp.exp(b_last + r_row - m_new)
            c_ref[0, hh] = a_c * cm + lax.dot_general((k * wk_col).astype(BF16), v, _TN,
                                                      preferred_element_type=F32)
            wk8 = jnp.broadcast_to(wk_row, (8, c)).astype(BF16)
            n_ref[0, hh] = a_c * nv + _dot(wk8, kb)[:1, :]
            m_ref[0, hh] = jnp.broadcast_to(m_new, (1, 128))

            main = hc * _sigmoid(og_ref[sl, vs])
            u_ref[sl, vs] = _head_norm_gate(main, g_ref[:, vs], z_ref[sl, vs]).astype(u_ref.dtype)


def _mlstm_prompt(qk, v, og, z, gates_t, bias, g_head, b, l):
    tl, hb = SCAN_TILE_L, SCAN_HEADS
    nt = l // tl
    bias3 = bias.reshape(2 * N_HEADS, 1, 1)
    nkb = MAIN_DK // (hb * HEAD_DK)
    nhb = N_HEADS // hb
    dk_spec = lambda off: pl.BlockSpec((tl, hb * HEAD_DK), lambda i, h, t: (i * nt + t, off + h))
    dv_spec = pl.BlockSpec((tl, hb * HEAD_DV), lambda i, h, t: (i * nt + t, h))
    return pl.pallas_call(
        _mlstm_prompt_body,
        grid=(b, nhb, nt),
        in_specs=[
            dk_spec(0), dk_spec(nkb), dv_spec, dv_spec, dv_spec,
            pl.BlockSpec((1, hb, 1, tl), lambda i, h, t: (i, h, 0, t)),
            pl.BlockSpec((1, hb, 1, tl), lambda i, h, t: (i, nhb + h, 0, t)),
            pl.BlockSpec((hb, 1, 1), lambda i, h, t: (h, 0, 0)),
            pl.BlockSpec((hb, 1, 1), lambda i, h, t: (nhb + h, 0, 0)),
            pl.BlockSpec((1, hb * HEAD_DV), lambda i, h, t: (0, h)),
        ],
        out_specs=[
            dv_spec,
            pl.BlockSpec((1, hb, HEAD_DK, HEAD_DV), lambda i, h, t: (i, h, 0, 0)),
            pl.BlockSpec((1, hb, 1, HEAD_DK), lambda i, h, t: (i, h, 0, 0)),
            pl.BlockSpec((1, hb, 1, 128), lambda i, h, t: (i, h, 0, 0)),
        ],
        out_shape=[jax.ShapeDtypeStruct((b * l, MAIN_DV), BF16),
                   jax.ShapeDtypeStruct((b, N_HEADS, HEAD_DK, HEAD_DV), F32),
                   jax.ShapeDtypeStruct((b, N_HEADS, 1, HEAD_DK), F32),
                   jax.ShapeDtypeStruct((b, N_HEADS, 1, 128), F32)],
        compiler_params=_cparams(("parallel", "parallel", "arbitrary")),
        name="mlstm_prompt",
    )(qk, qk, v, og, z, gates_t, gates_t, bias3, bias3, g_head.reshape(1, MAIN_DV))


def _memattn_prompt_body(q_ref, k_ref, v_ref, z_ref, u_ref):
    q = (q_ref[...] * MEM_HEAD_DIM ** -0.5).astype(BF16)
    s = lax.dot_general(q, k_ref[...].astype(BF16), _NT, preferred_element_type=F32)
    p = jnp.exp(s - jnp.max(s, axis=-1, keepdims=True))
    p = p * (1.0 / jnp.sum(p, axis=-1, keepdims=True))
    o = _dot(p.astype(BF16), v_ref[...].astype(BF16))
    z = z_ref[...]
    u_ref[...] = (o * (z * _sigmoid(z))).astype(u_ref.dtype)


def _memattn_prompt(qm, mk, mv, z, b, l):
    tl = SCAN_TILE_L
    nt = l // tl
    zoff = MAIN_DV // MEM_HEAD_DIM
    return pl.pallas_call(
        _memattn_prompt_body,
        grid=(b, N_MEM_HEADS, nt),
        in_specs=[
            pl.BlockSpec((tl, MEM_HEAD_DIM), lambda i, h, t: (i * nt + t, h)),
            pl.BlockSpec((N_MEM, MEM_HEAD_DIM), lambda i, h, t: (i, h)),
            pl.BlockSpec((N_MEM, MEM_HEAD_DIM), lambda i, h, t: (i, h)),
            pl.BlockSpec((tl, MEM_HEAD_DIM), lambda i, h, t: (i * nt + t, zoff + h)),
        ],
        out_specs=pl.BlockSpec((tl, MEM_HEAD_DIM), lambda i, h, t: (i * nt + t, h)),
        out_shape=jax.ShapeDtypeStruct((b * l, MEM_DIM), BF16),
        compiler_params=_cparams(("parallel", "parallel", "parallel")),
        name="memattn_prompt",
    )(qm, mk, mv, z)


def _col_bcast(row16, width):
    ones = jnp.ones((BF16_SUBLANES, width), BF16)
    return lax.dot_general(row16, ones, _TN, preferred_element_type=F32)


def _first_row16(x):
    n = x.shape[1]
    r = lax.broadcasted_iota(jnp.int32, (BF16_SUBLANES, n), 0)
    return jnp.where(r == 0, jnp.broadcast_to(x, (BF16_SUBLANES, n)), 0.0).astype(BF16)


def _decode_body(a_ref, w_ref, e_ref, qk_ref, v_ref, z_ref, g_ref, s_ref, *rest,
                 use_rope, is_mlstm):
    if use_rope:
        cos_ref, sin_ref = rest[:2]
        rest = rest[2:]
    if is_mlstm:
        og_ref, n_ref, u_ref, so_ref, no_ref = rest
    else:
        u_ref, so_ref = rest
    base = pl.program_id(0) * N_HEADS
    for h in range(N_HEADS):
        a = a_ref[base + h]
        w = w_ref[base + h]
        qs = slice(h * HEAD_DK, (h + 1) * HEAD_DK)
        ks = slice(MAIN_DK + h * HEAD_DK, MAIN_DK + (h + 1) * HEAD_DK)
        vs = slice(h * HEAD_DV, (h + 1) * HEAD_DV)
        q = qk_ref[0, :, qs]
        k = qk_ref[0, :, ks]
        if use_rope:
            q = _rope(q, cos_ref[...], sin_ref[...])
            k = _rope(k, cos_ref[...], sin_ref[...])
        k = k * HEAD_DK ** -0.5
        v = v_ref[0, :, vs]
        vf = v.astype(F32)
        qb = q.astype(BF16)
        kb = k.astype(BF16)
        s = s_ref[0, h]

        qk = jnp.sum(qb.astype(F32) * kb.astype(F32), axis=1, keepdims=True)
        sc = qk * w
        q_cols = _col_bcast(_first_row16(q), HEAD_DV)
        qs_ = jnp.sum(q_cols * s, axis=0, keepdims=True)
        num = sc.astype(BF16).astype(F32) * vf + a * qs_
        kv = lax.dot_general(_first_row16(k * w), jnp.broadcast_to(v, (BF16_SUBLANES, HEAD_DV)),
                             _TN, preferred_element_type=F32)
        so_ref[0, h] = a * s + kv

        if is_mlstm:
            nv = n_ref[0, h]
            qn = jnp.sum(qb.astype(F32) * nv.astype(BF16).astype(F32), axis=1, keepdims=True)
            den = sc + a * qn
            main = num / jnp.maximum(jnp.abs(den), e_ref[base + h])
            wb = (jnp.zeros((1, 1), F32) + w).astype(BF16).astype(F32)
            no_ref[0, h] = a * nv + wb * kb.astype(F32)
            main = main * _sigmoid(og_ref[0, :, vs])
        else:
            main = num
        u_ref[0, :, vs] = _head_norm_gate(main, g_ref[:, vs], z_ref[0, :, vs]).astype(u_ref.dtype)


def _decode(a, w, e, qk, v, z, g_head, state, rope_tabs=None, og=None, n_state=None):
    b = qk.shape[0]
    is_mlstm = og is not None
    use_rope = rope_tabs is not None
    qk_spec = pl.BlockSpec((1, 1, 2 * MAIN_DK), lambda i, *_: (i, 0, 0))
    dv_spec = pl.BlockSpec((1, 1, MAIN_DV), lambda i, *_: (i, 0, 0))
    st_spec = pl.BlockSpec((1, N_HEADS, HEAD_DK, HEAD_DV), lambda i, *_: (i, 0, 0, 0))
    n_spec = pl.BlockSpec((1, N_HEADS, 1, HEAD_DK), lambda i, *_: (i, 0, 0, 0))
    in_specs = [qk_spec, dv_spec, dv_spec,
                pl.BlockSpec((1, MAIN_DV), lambda i, *_: (0, 0)), st_spec]
    args = [qk, v, z, g_head.reshape(1, MAIN_DV), state]
    out_specs = [dv_spec, st_spec]
    out_shape = [jax.ShapeDtypeStruct((b, 1, MAIN_DV), BF16),
                 jax.ShapeDtypeStruct(state.shape, F32)]
    if use_rope:
        tab = pl.BlockSpec((1, HEAD_DK // 2), lambda i, *_: (0, 0))
        in_specs += [tab, tab]
        args += list(rope_tabs)
    if is_mlstm:
        in_specs += [dv_spec, n_spec]
        args += [og, n_state]
        out_specs.append(n_spec)
        out_shape.append(jax.ShapeDtypeStruct(n_state.shape, F32))
    return pl.pallas_call(
        functools.partial(_decode_body, use_rope=use_rope, is_mlstm=is_mlstm),
        grid_spec=pltpu.PrefetchScalarGridSpec(
            num_scalar_prefetch=3, grid=(b,),
            in_specs=in_specs, out_specs=out_specs),
        out_shape=out_shape,
        compiler_params=_cparams(("parallel",)),
        name="mlstm_decode" if is_mlstm else "ret_decode",
    )(a, w, e, *args)


def _gate_decode_body(gi_ref, gf_ref, bi_ref, bf_ref, m_ref, mo_ref, a_ref, w_ref, e_ref):
    ic = gi_ref[...] + bi_ref[...]
    lf = _log_sigmoid(gf_ref[...] + bf_ref[...])
    log_prev = lf + m_ref[...]
    m_t = jnp.maximum(log_prev, ic)
    mo_ref[...] = m_t
    a_ref[...] = jnp.exp(log_prev - m_t)
    w_ref[...] = jnp.exp(ic - m_t)
    e_ref[...] = jnp.exp(-m_t)


def _gate_decode(g_i, g_f, b_i, b_f, m):
    sds = jax.ShapeDtypeStruct(m.shape, F32)
    return pl.pallas_call(
        _gate_decode_body, out_shape=[sds, sds, sds, sds], name="mlstm_decode_gates",
    )(g_i, g_f, b_i.reshape(1, N_HEADS), b_f.reshape(1, N_HEADS), m)


def _lane_tiled(x):
    lead = x.shape[:-2]
    nh, d = x.shape[-2:]
    x = x.reshape(lead + (nh, d // 128, 128))
    x = jnp.swapaxes(x, -3, -2)
    return x.reshape(lead + (d // 128 * nh, 128))


def _memattn_decode_body(q_ref, k_ref, v_ref, z_ref, u_ref):
    rows = MEM_HEAD_DIM // 128 * N_MEM_HEADS
    for bi in range(MEMDEC_ROWS):
        q = q_ref[bi] * MEM_HEAD_DIM ** -0.5
        prod = k_ref[bi] * q[None]
        part = prod[:, :rows // 2, :] + prod[:, rows // 2:, :]
        part = part + pltpu.roll(part, N_MEM_HEADS, 1)
        s = jnp.sum(part, axis=-1, keepdims=True)
        p = jnp.exp(s - jnp.max(s, axis=0, keepdims=True))
        p = p / jnp.sum(p, axis=0, keepdims=True)
        o = jnp.concatenate([jnp.sum(p * v_ref[bi, :, :rows // 2, :], axis=0),
                             jnp.sum(p * v_ref[bi, :, rows // 2:, :], axis=0)], axis=0)
        z = z_ref[bi]
        u_ref[bi] = (o * (z * _sigmoid(z))).astype(u_ref.dtype)


def _memattn_decode(qm, cache_k, cache_v, z, layer):
    b = qm.shape[0]
    nb = MEMDEC_ROWS
    rows = qm.shape[1]
    row = pl.BlockSpec((nb, rows, 128), lambda i: (i, 0, 0))
    kv = pl.BlockSpec((None, nb, N_MEM, rows, 128), lambda i: (layer, i, 0, 0, 0))
    return pl.pallas_call(
        _memattn_decode_body,
        grid=(b // nb,),
        in_specs=[row, kv, kv, row],
        out_specs=row,
        out_shape=jax.ShapeDtypeStruct((b, rows, 128), BF16),
        compiler_params=_cparams(("parallel",)),
        name=f"memattn_decode_{layer}",
    )(qm, cache_k, cache_v, z)


def _from_lane_tiled(x, nh):
    lead = x.shape[:-2]
    nt = x.shape[-2] // nh
    x = x.reshape(lead + (nt, nh, 128))
    x = jnp.swapaxes(x, -3, -2)
    return x.reshape(lead + (nh, nt * 128))


def _rope_tables(pos):
    d = HEAD_DK
    inv = 1.0 / (ROPE_THETA ** (jnp.arange(0, d, 2, dtype=F32) / d))
    ang = pos[:, None] * inv[None, :]
    return jnp.cos(ang), jnp.sin(ang)


_REGIONS = (("qk", COL_QK, COL_V, F32), ("v", COL_V, COL_QM, BF16), ("qm", COL_QM, COL_Z, F32),
            ("z", COL_Z, COL_OG, F32), ("og", COL_OG, COL_GATES, F32))


def _in_proj(h, w, tag, with_mlstm):
    w_is_nk = with_mlstm
    wk = jnp.swapaxes(w, 1, 2) if w_is_nk else w
    out = {}
    for name, c0, c1, dt in _REGIONS:
        if name == "og" and not with_mlstm:
            continue
        out[name] = _inproj_stream(h, wk, 0, c0, c1 - c0, dt, f"inproj_{name}_{tag}", w_is_nk)
    if with_mlstm:
        wg = jnp.pad(w[0, :, COL_GATES:], ((0, 0), (0, 128 - 2 * N_HEADS))).astype(BF16)
        out["gates"] = _matmul(h, wg, F32, f"inproj_gates_{tag}")
    return out


def kernel(x_prompt, x_sample, mem_prompt, state_ret, state_mlstm_C, state_mlstm_n, state_mlstm_m,
           cache_mem_k, cache_mem_v, g_pre, g_post, w_in_ret, w_in_mlstm, b_gate_mlstm, g_head,
           w_out, g_mem, w_mem_kv):
    bp, lp, _ = x_prompt.shape
    bs = x_sample.shape[0]
    mp = bp * lp

    mem2d = mem_prompt.reshape(bp * N_MEM, D_MODEL)
    mem_k, mem_v = [], []
    for l in range(DEPTH):
        hm = _rmsnorm(mem2d, g_mem[l], f"mem_norm_{l}")
        mem_k.append(_matmul_w32(hm, w_mem_kv, l, 0, MEM_DIM, F32, f"mem_k_{l}"))
        mem_v.append(_matmul_w32(hm, w_mem_kv, l, MEM_DIM, MEM_DIM, F32, f"mem_v_{l}"))

    cos_p, sin_p = _rope_tables(jnp.arange(lp, dtype=F32))
    cos_s, sin_s = _rope_tables(PAST_LEN + jnp.arange(1, dtype=F32))
    bias = b_gate_mlstm[0]
    xp = x_prompt.reshape(mp, D_MODEL)
    xs = x_sample.reshape(bs, D_MODEL)
    srow = lambda a: a[mp:].reshape(bs, 1, a.shape[-1])
    smem = lambda a: _lane_tiled(a[mp:].reshape(bs, N_MEM_HEADS, MEM_HEAD_DIM))
    cache_k = _lane_tiled(cache_mem_k)
    cache_v = _lane_tiled(cache_mem_v)
    unmem = lambda u: _from_lane_tiled(u, N_MEM_HEADS).reshape(bs, MEM_DIM)

    h = _rmsnorm_stacked(xp, xs, g_pre[0])
    pr = _in_proj(h, w_in_ret, "0", False)
    u_main_p, ret_prompt = _ret_prompt(pr["qk"], pr["v"], pr["z"], cos_p, sin_p, g_head[0], bp, lp)
    u_mem_p = _memattn_prompt(pr["qm"], mem_k[0], mem_v[0], pr["z"], bp, lp)
    gamma = 1.0 - 2.0 ** (-5.0 - jnp.arange(N_HEADS, dtype=F32))
    a_ret = jnp.exp(jnp.log(gamma))
    a_ret = jnp.broadcast_to(a_ret[None, :], (bs, N_HEADS)).reshape(-1)
    ones = jnp.ones((bs * N_HEADS,), F32)
    u_main_s, ret_sample = _decode(a_ret, ones, ones, srow(pr["qk"]), srow(pr["v"]), srow(pr["z"]),
                                   g_head[0], state_ret[0], rope_tabs=(cos_s, sin_s))
    u_mem_s = _memattn_decode(smem(pr["qm"]), cache_k, cache_v, smem(pr["z"][:, MAIN_DV:]), 0)
    out_p, out_s = _outproj_stream(u_main_p, u_mem_p, u_main_s.reshape(bs, MAIN_DV),
                                   unmem(u_mem_s), w_out, 0)
    x1, h = _post_mid(out_p, out_s, xp, xs, g_post[0], g_pre[1])

    pr = _in_proj(h, w_in_mlstm, "1", True)
    gates = pr["gates"][:, :2 * N_HEADS]
    gates_t = gates[:mp].reshape(bp, lp, 2 * N_HEADS).transpose(0, 2, 1)[:, :, None, :]
    u_main_p, c_prompt, n_prompt, m_prompt = _mlstm_prompt(
        pr["qk"], pr["v"], pr["og"], pr["z"], gates_t, bias, g_head[1], bp, lp)
    u_mem_p = _memattn_prompt(pr["qm"], mem_k[1], mem_v[1], pr["z"], bp, lp)

    g_i = gates[mp:, :N_HEADS]
    g_f = gates[mp:, N_HEADS:]
    m_new, a_m, w_m, e_m = _gate_decode(g_i, g_f, bias[:N_HEADS], bias[N_HEADS:], state_mlstm_m[0])
    u_main_s, c_sample, n_sample = _decode(
        a_m.reshape(-1), w_m.reshape(-1), e_m.reshape(-1), srow(pr["qk"]), srow(pr["v"]),
        srow(pr["z"]), g_head[1], state_mlstm_C[0], og=srow(pr["og"]),
        n_state=state_mlstm_n[0].reshape(bs, N_HEADS, 1, HEAD_DK))
    u_mem_s = _memattn_decode(smem(pr["qm"]), cache_k, cache_v, smem(pr["z"][:, MAIN_DV:]), 1)
    out_p, out_s = _outproj_stream(u_main_p, u_mem_p, u_main_s.reshape(bs, MAIN_DV),
                                   unmem(u_mem_s), w_out, 1)
    y_prompt = _post_last(out_p, x1, 0, g_post[1], "post_p1")
    y_sample = _post_last(out_s, x1, mp, g_post[1], "post_s1")

    shape_kv = (DEPTH, bp, N_MEM, N_MEM_HEADS, MEM_HEAD_DIM)
    return (
        y_prompt.reshape(bp, lp, D_MODEL),
        y_sample.reshape(bs, 1, D_MODEL),
        ret_prompt[None],
        c_prompt[None],
        n_prompt.reshape(1, bp, N_HEADS, HEAD_DK),
        m_prompt[:, :, 0, 0][None],
        jnp.stack(mem_k).reshape(shape_kv),
        jnp.stack(mem_v).reshape(shape_kv),
        ret_sample[None],
        c_sample[None],
        n_sample.reshape(1, bs, N_HEADS, HEAD_DK),
        m_new[None],
    )
```

```python
import functools

import jax
import jax.numpy as jnp
from jax import lax
from jax.experimental import pallas as pl
from jax.experimental.pallas import tpu as pltpu

F32 = jnp.float32
BF16 = jnp.bfloat16

D_MODEL = 4096
DEPTH = 2
PAST_LEN = 16384
MIX_WIDTH = 2 * D_MODEL
N_HEADS = 12
HEAD_DV = MIX_WIDTH // 16
HEAD_DK = HEAD_DV // 2
MAIN_DK = N_HEADS * HEAD_DK
MAIN_DV = N_HEADS * HEAD_DV
N_MEM = 256
N_MEM_HEADS = 4
MEM_HEAD_DIM = MIX_WIDTH // 16
MEM_DIM = N_MEM_HEADS * MEM_HEAD_DIM
CHUNK = 128
ROPE_THETA = 10000.0
EPS = 1e-6
NEG_INF = -1e30

COL_QK = 0
COL_V = 2 * MAIN_DK
COL_QM = COL_V + MAIN_DV
COL_Z = COL_QM + MEM_DIM
COL_OG = COL_Z + MIX_WIDTH
COL_GATES = COL_OG + MAIN_DV

VMEM_LIMIT_BYTES = 56 * 1024 * 1024
BF16_SUBLANES = 16
MM_MAX_TILE_M = 640
MM_TILE_N = 1024
MM_VMEM_LIMIT_BYTES = 60 * 1024 * 1024
OUT_TILE_M = 512
OUT_TILE_N = 512
SCAN_TILE_L = 512
RET_SCAN_HEADS = 4
MLSTM_SCAN_HEADS = 2
MEMATTN_TILE_L = 1024
ROW_TILE = 128
MEMDEC_ROWS = 4

_NT = (((1,), (1,)), ((), ()))
_TN = (((0,), (0,)), ((), ()))


def _cparams(sem, vmem_limit_bytes=VMEM_LIMIT_BYTES):
    return pltpu.CompilerParams(dimension_semantics=sem, vmem_limit_bytes=vmem_limit_bytes)


def _row_tile(m, limit):
    best = None
    for t in range(BF16_SUBLANES, limit + 1, BF16_SUBLANES):
        if m % t == 0:
            best = t
    assert best is not None, m
    return best


def _sigmoid(x):
    return 0.5 * jnp.tanh(0.5 * x) + 0.5


def _log_sigmoid(x):
    return jnp.minimum(x, 0.0) - jnp.log(1.0 + jnp.exp(-jnp.abs(x)))


def _rope(x, cos, sin):
    half = HEAD_DK // 2
    x1, x2 = x[:, :half], x[:, half:]
    return jnp.concatenate([x1 * cos - x2 * sin, x1 * sin + x2 * cos], axis=-1)


def _head_norm_gate(main, g, z):
    mu = jnp.mean(main, axis=-1, keepdims=True)
    d = main - mu
    var = jnp.mean(d * d, axis=-1, keepdims=True)
    hn = d * lax.rsqrt(var + EPS)
    return hn * g * (z * _sigmoid(z))


def _rms(x, g):
    ms = jnp.mean(x * x, axis=-1, keepdims=True)
    return x * lax.rsqrt(ms + EPS) * g


def _dot(a, b):
    return jnp.dot(a, b, preferred_element_type=F32)


def _rmsnorm_body(x_ref, g_ref, o_ref):
    o_ref[...] = _rms(x_ref[...], g_ref[...]).astype(o_ref.dtype)


def _rmsnorm(x, g, name):
    m, d = x.shape
    tm = ROW_TILE
    return pl.pallas_call(
        _rmsnorm_body,
        grid=(m // tm,),
        in_specs=[pl.BlockSpec((tm, d), lambda i: (i, 0)),
                  pl.BlockSpec((1, d), lambda i: (0, 0))],
        out_specs=pl.BlockSpec((tm, d), lambda i: (i, 0)),
        out_shape=jax.ShapeDtypeStruct((m, d), BF16),
        compiler_params=_cparams(("parallel",)),
        name=name,
    )(x, g.reshape(1, d))


def _rmsnorm_stacked_body(xp_ref, xs_ref, g_ref, o_ref, *, n_prompt_tiles):
    i = pl.program_id(0)

    @pl.when(i < n_prompt_tiles)
    def _():
        o_ref[...] = _rms(xp_ref[...], g_ref[...]).astype(o_ref.dtype)

    @pl.when(i >= n_prompt_tiles)
    def _():
        o_ref[...] = _rms(xs_ref[...], g_ref[...]).astype(o_ref.dtype)


def _rmsnorm_stacked(xp, xs, g):
    mp, d = xp.shape
    ms = xs.shape[0]
    tm = ROW_TILE
    npt, nst = mp // tm, ms // tm
    return pl.pallas_call(
        functools.partial(_rmsnorm_stacked_body, n_prompt_tiles=npt),
        grid=(npt + nst,),
        in_specs=[pl.BlockSpec((tm, d), lambda i: (jnp.minimum(i, npt - 1), 0)),
                  pl.BlockSpec((tm, d), lambda i: (jnp.maximum(i - npt, 0), 0)),
                  pl.BlockSpec((1, d), lambda i: (0, 0))],
        out_specs=pl.BlockSpec((tm, d), lambda i: (i, 0)),
        out_shape=jax.ShapeDtypeStruct((mp + ms, d), BF16),
        compiler_params=_cparams(("arbitrary",)),
        name="pre_norm_0",
    )(xp, xs, g.reshape(1, d))


def _mm_w32_body(a_ref, w_ref, o_ref, wb_ref, *, w_is_nk):
    @pl.when(pl.program_id(1) == 0)
    def _():
        wb_ref[...] = w_ref[...].astype(BF16)

    if w_is_nk:
        acc = lax.dot_general(a_ref[...], wb_ref[...], _NT, preferred_element_type=F32)
    else:
        acc = _dot(a_ref[...], wb_ref[...])
    o_ref[...] = acc.astype(o_ref.dtype)


def _matmul_w32(a, w, layer, col0, ncols, out_dtype, name, w_is_nk=False):
    m, k = a.shape
    tm = _row_tile(m, MM_MAX_TILE_M)
    tn = min(MM_TILE_N, ncols)
    assert col0 % tn == 0 and ncols % tn == 0
    off = col0 // tn
    if w_is_nk:
        w_spec = pl.BlockSpec((None, tn, k), lambda j, i: (layer, off + j, 0))
        wb_shape = (tn, k)
    else:
        w_spec = pl.BlockSpec((None, k, tn), lambda j, i: (layer, 0, off + j))
        wb_shape = (k, tn)
    return pl.pallas_call(
        functools.partial(_mm_w32_body, w_is_nk=w_is_nk),
        grid=(ncols // tn, m // tm),
        in_specs=[pl.BlockSpec((tm, k), lambda j, i: (i, 0)), w_spec],
        out_specs=pl.BlockSpec((tm, tn), lambda j, i: (i, j)),
        out_shape=jax.ShapeDtypeStruct((m, ncols), out_dtype),
        scratch_shapes=[pltpu.VMEM(wb_shape, BF16)],
        compiler_params=_cparams(("parallel", "arbitrary"), MM_VMEM_LIMIT_BYTES),
        name=name,
    )(a, w)


def _mm_body(a_ref, w_ref, o_ref):
    o_ref[...] = _dot(a_ref[...], w_ref[...]).astype(o_ref.dtype)


def _matmul(a, w, out_dtype, name):
    m, k = a.shape
    n = w.shape[1]
    tm = _row_tile(m, MM_MAX_TILE_M)
    return pl.pallas_call(
        _mm_body,
        grid=(m // tm,),
        in_specs=[pl.BlockSpec((tm, k), lambda i: (i, 0)),
                  pl.BlockSpec((k, n), lambda i: (0, 0))],
        out_specs=pl.BlockSpec((tm, n), lambda i: (i, 0)),
        out_shape=jax.ShapeDtypeStruct((m, n), out_dtype),
        compiler_params=_cparams(("parallel",)),
        name=name,
    )(a, w)


def _outproj_body(um_ref, ue_ref, w_ref, o_ref):
    o_ref[...] = (_dot(um_ref[...], w_ref[:MAIN_DV, :]) + _dot(ue_ref[...], w_ref[MAIN_DV:, :]))


def _outproj(u_main, u_mem, w, name):
    m = u_main.shape[0]
    n = w.shape[1]
    tm = _row_tile(m, OUT_TILE_M)
    tn = OUT_TILE_N
    return pl.pallas_call(
        _outproj_body,
        grid=(m // tm, n // tn),
        in_specs=[pl.BlockSpec((tm, MAIN_DV), lambda i, j: (i, 0)),
                  pl.BlockSpec((tm, MEM_DIM), lambda i, j: (i, 0)),
                  pl.BlockSpec((MIX_WIDTH, tn), lambda i, j: (0, j))],
        out_specs=pl.BlockSpec((tm, tn), lambda i, j: (i, j)),
        out_shape=jax.ShapeDtypeStruct((m, n), F32),
        compiler_params=_cparams(("parallel", "parallel")),
        name=name,
    )(u_main, u_mem, w)


def _post_mid_body(op_ref, os_ref, xp_ref, xs_ref, gp_ref, gn_ref, y_ref, h_ref, *, n_prompt_tiles):
    i = pl.program_id(0)

    def emit(o_ref, x_ref):
        y = x_ref[...] + _rms(o_ref[...], gp_ref[...])
        y_ref[...] = y
        h_ref[...] = _rms(y, gn_ref[...]).astype(h_ref.dtype)

    @pl.when(i < n_prompt_tiles)
    def _():
        emit(op_ref, xp_ref)

    @pl.when(i >= n_prompt_tiles)
    def _():
        emit(os_ref, xs_ref)


def _post_mid(out_p, out_s, x_p, x_s, g_post, g_next):
    mp, d = x_p.shape
    ms = x_s.shape[0]
    tm = ROW_TILE
    npt, nst = mp // tm, ms // tm
    prow = pl.BlockSpec((tm, d), lambda i: (jnp.minimum(i, npt - 1), 0))
    srow = pl.BlockSpec((tm, d), lambda i: (jnp.maximum(i - npt, 0), 0))
    vec = pl.BlockSpec((1, d), lambda i: (0, 0))
    orow = pl.BlockSpec((tm, d), lambda i: (i, 0))
    return pl.pallas_call(
        functools.partial(_post_mid_body, n_prompt_tiles=npt),
        grid=(npt + nst,),
        in_specs=[prow, srow, prow, srow, vec, vec],
        out_specs=[orow, orow],
        out_shape=[jax.ShapeDtypeStruct((mp + ms, d), F32),
                   jax.ShapeDtypeStruct((mp + ms, d), BF16)],
        compiler_params=_cparams(("arbitrary",)),
        name="post_0",
    )(out_p, out_s, x_p, x_s, g_post.reshape(1, d), g_next.reshape(1, d))


def _post_last_body(o_ref, x_ref, gp_ref, y_ref):
    y_ref[...] = x_ref[...] + _rms(o_ref[...], gp_ref[...])


def _post_last(out, x_all, row0, g_post, name):
    m, d = out.shape
    tm = ROW_TILE
    off = row0 // tm
    return pl.pallas_call(
        _post_last_body, grid=(m // tm,),
        in_specs=[pl.BlockSpec((tm, d), lambda i: (i, 0)),
                  pl.BlockSpec((tm, d), lambda i: (off + i, 0)),
                  pl.BlockSpec((1, d), lambda i: (0, 0))],
        out_specs=pl.BlockSpec((tm, d), lambda i: (i, 0)),
        out_shape=jax.ShapeDtypeStruct((m, d), F32),
        compiler_params=_cparams(("parallel",)), name=name,
    )(out, x_all, g_post.reshape(1, d))


def _wcast_blocks(n_steps):
    nblk = 1
    while nblk * 2 <= n_steps:
        nblk *= 2
    return nblk


def _wcast_specs(layer, grid, n_rows, n_cols):
    n_steps = grid[0] * grid[1] * grid[2]
    nblk = _wcast_blocks(n_steps)
    rows = n_rows // nblk
    assert rows * nblk == n_rows and rows % BF16_SUBLANES == 0

    def blk(i, h, t):
        return jnp.minimum((i * grid[1] + h) * grid[2] + t, nblk - 1)

    return (pl.BlockSpec((None, rows, n_cols), lambda i, h, t: (layer, blk(i, h, t), 0)),
            pl.BlockSpec((rows, n_cols), lambda i, h, t: (blk(i, h, t), 0)))


def _ret_prompt_body(q_ref, k_ref, v_ref, z_ref, cos_ref, sin_ref, din_ref, qd_ref, kd_ref,
                     cd_ref, g_ref, w32_ref, u_ref, s_ref, w16_ref):
    w16_ref[...] = w32_ref[...].astype(BF16)

    @pl.when(pl.program_id(2) == 0)
    def _():
        s_ref[...] = jnp.zeros_like(s_ref)

    for ci in range(SCAN_TILE_L // CHUNK):
        sl = slice(ci * CHUNK, (ci + 1) * CHUNK)
        cos, sin = cos_ref[sl, :], sin_ref[sl, :]
        for hh in range(RET_SCAN_HEADS):
            ks = slice(hh * HEAD_DK, (hh + 1) * HEAD_DK)
            vs = slice(hh * HEAD_DV, (hh + 1) * HEAD_DV)
            din = din_ref[hh]
            qd = qd_ref[hh]
            kd = kd_ref[hh]
            cd = cd_ref[hh][:, :1]
            q = _rope(q_ref[sl, ks], cos, sin)
            k = _rope(k_ref[sl, ks], cos, sin) * HEAD_DK ** -0.5
            v = v_ref[sl, vs]
            s = s_ref[0, hh]
            sc = lax.dot_general(q.astype(BF16), k.astype(BF16), _NT,
                                 preferred_element_type=F32) * din
            o = _dot(sc.astype(BF16), v) + _dot((q * qd).astype(BF16), s.astype(BF16))
            s_ref[0, hh] = s * cd + lax.dot_general((k * kd).astype(BF16), v, _TN,
                                                    preferred_element_type=F32)
            u_ref[sl, vs] = _head_norm_gate(o, g_ref[:, vs], z_ref[sl, vs]).astype(u_ref.dtype)


def _ret_prompt(qk, v, z, cos, sin, g_head, w_out, layer, b, l):
    tl, c, hb = SCAN_TILE_L, CHUNK, RET_SCAN_HEADS
    nt = l // tl
    grid = (b, N_HEADS // hb, nt)
    w32_spec, w16_spec = _wcast_specs(layer, grid, w_out.shape[1], w_out.shape[2])
    lg = jnp.log(1.0 - 2.0 ** (-5.0 - jnp.arange(N_HEADS, dtype=F32)))
    idx = jnp.arange(c, dtype=F32)
    diff = idx[:, None] - idx[None, :]
    decay_in = jnp.where(diff[None] >= 0.0,
                         jnp.exp(lg[:, None, None] * jnp.maximum(diff, 0.0)[None]), 0.0)
    q_dec = jnp.exp(lg[:, None] * (idx + 1.0)[None, :])[:, :, None]
    k_dec = jnp.exp(lg[:, None] * (c - 1.0 - idx)[None, :])[:, :, None]
    chunk_dec = jnp.broadcast_to(jnp.exp(lg * c)[:, None, None], (N_HEADS, 1, 128))
    nkb = MAIN_DK // (hb * HEAD_DK)
    return pl.pallas_call(
        _ret_prompt_body,
        grid=grid,
        in_specs=[
            pl.BlockSpec((tl, hb * HEAD_DK), lambda i, h, t: (i * nt + t, h)),
            pl.BlockSpec((tl, hb * HEAD_DK), lambda i, h, t: (i * nt + t, nkb + h)),
            pl.BlockSpec((tl, hb * HEAD_DV), lambda i, h, t: (i * nt + t, h)),
            pl.BlockSpec((tl, hb * HEAD_DV), lambda i, h, t: (i * nt + t, h)),
            pl.BlockSpec((tl, HEAD_DK // 2), lambda i, h, t: (t, 0)),
            pl.BlockSpec((tl, HEAD_DK // 2), lambda i, h, t: (t, 0)),
            pl.BlockSpec((hb, c, c), lambda i, h, t: (h, 0, 0)),
            pl.BlockSpec((hb, c, 1), lambda i, h, t: (h, 0, 0)),
            pl.BlockSpec((hb, c, 1), lambda i, h, t: (h, 0, 0)),
            pl.BlockSpec((hb, 1, 128), lambda i, h, t: (h, 0, 0)),
            pl.BlockSpec((1, hb * HEAD_DV), lambda i, h, t: (0, h)),
            w32_spec,
        ],
        out_specs=[
            pl.BlockSpec((tl, hb * HEAD_DV), lambda i, h, t: (i * nt + t, h)),
            pl.BlockSpec((1, hb, HEAD_DK, HEAD_DV), lambda i, h, t: (i, h, 0, 0)),
            w16_spec,
        ],
        out_shape=[jax.ShapeDtypeStruct((b * l, MAIN_DV), BF16),
                   jax.ShapeDtypeStruct((b, N_HEADS, HEAD_DK, HEAD_DV), F32),
                   jax.ShapeDtypeStruct(w_out.shape[1:], BF16)],
        compiler_params=_cparams(("arbitrary", "arbitrary", "arbitrary")),
        name="ret_prompt",
    )(qk, qk, v, z, cos, sin, decay_in, q_dec, k_dec, chunk_dec, g_head.reshape(1, MAIN_DV),
      w_out)


def _col_from_row(row, eye):
    return jnp.sum(jnp.where(eye, row, 0.0), axis=1, keepdims=True)


def _mlstm_prompt_body(q_ref, k_ref, v_ref, og_ref, z_ref, gi_ref, gf_ref, bi_ref, bf_ref, g_ref,
                       w32_ref, u_ref, c_ref, n_ref, m_ref, w16_ref):
    w16_ref[...] = w32_ref[...].astype(BF16)

    @pl.when(pl.program_id(2) == 0)
    def _():
        c_ref[...] = jnp.zeros_like(c_ref)
        n_ref[...] = jnp.zeros_like(n_ref)
        m_ref[...] = jnp.full_like(m_ref, NEG_INF)

    c = CHUNK
    ri = lax.broadcasted_iota(jnp.int32, (c, c), 0)
    cj = lax.broadcasted_iota(jnp.int32, (c, c), 1)
    eye = ri == cj
    causal = ri >= cj
    lane8 = lax.broadcasted_iota(jnp.int32, (8, c), 1)
    for ci in range(SCAN_TILE_L // CHUNK):
        sl = slice(ci * CHUNK, (ci + 1) * CHUNK)
        for hh in range(MLSTM_SCAN_HEADS):
            ks = slice(hh * HEAD_DK, (hh + 1) * HEAD_DK)
            vs = slice(hh * HEAD_DV, (hh + 1) * HEAD_DV)
            ic = gi_ref[0, hh, :, sl] + bi_ref[hh]
            lf = _log_sigmoid(gf_ref[0, hh, :, sl] + bf_ref[hh])
            cs = jnp.broadcast_to(lf, (8, c))
            sh = 1
            while sh < c:
                cs = cs + jnp.where(lane8 >= sh, pltpu.roll(cs, sh, 1), 0.0)
                sh *= 2
            b_row = cs[:1, :]
            r_row = ic - b_row
            b_col = _col_from_row(b_row, eye)
            r_col = _col_from_row(r_row, eye)
            b_last = b_row[:, c - 1:]
            m_prev = m_ref[0, hh][:, :1]

            logw = jnp.where(causal, b_col + r_row, NEG_INF)
            log_prev = b_col + m_prev
            m_t = jnp.maximum(log_prev, jnp.max(logw, axis=1, keepdims=True))
            w = jnp.exp(logw - m_t)
            a_prev = jnp.exp(log_prev - m_t)

            q = q_ref[sl, ks]
            k = k_ref[sl, ks] * HEAD_DK ** -0.5
            v = v_ref[sl, vs]
            qb, kb = q.astype(BF16), k.astype(BF16)
            cm = c_ref[0, hh]
            nv = n_ref[0, hh]
            sc = lax.dot_general(qb, kb, _NT, preferred_element_type=F32) * w
            num = _dot(sc.astype(BF16), v) + a_prev * _dot(qb, cm.astype(BF16))
            qn = jnp.sum(qb.astype(F32) * nv.astype(BF16).astype(F32), axis=1, keepdims=True)
            den = jnp.sum(sc, axis=1, keepdims=True) + a_prev * qn
            hc = num * (1.0 / jnp.maximum(jnp.abs(den), jnp.exp(-m_t)))

            m_new = m_t[c - 1:, :]
            a_c = jnp.exp(b_last + m_prev - m_new)
            wk_col = jnp.exp(b_last + r_col - m_new)
            wk_row = jnp.exp(b_last + r_row - m_new)
            c_ref[0, hh] = a_c * cm + lax.dot_general((k * wk_col).astype(BF16), v, _TN,
                                                      preferred_element_type=F32)
            wk8 = jnp.broadcast_to(wk_row, (8, c)).astype(BF16)
            n_ref[0, hh] = a_c * nv + _dot(wk8, kb)[:1, :]
            m_ref[0, hh] = jnp.broadcast_to(m_new, (1, 128))

            main = hc * _sigmoid(og_ref[sl, vs])
            u_ref[sl, vs] = _head_norm_gate(main, g_ref[:, vs], z_ref[sl, vs]).astype(u_ref.dtype)


def _mlstm_prompt(qk, v, og, z, gates_t, bias, g_head, w_out, layer, b, l):
    tl, hb = SCAN_TILE_L, MLSTM_SCAN_HEADS
    nt = l // tl
    bias3 = bias.reshape(2 * N_HEADS, 1, 1)
    nkb = MAIN_DK // (hb * HEAD_DK)
    nhb = N_HEADS // hb
    dk_spec = lambda off: pl.BlockSpec((tl, hb * HEAD_DK), lambda i, h, t: (i * nt + t, off + h))
    dv_spec = pl.BlockSpec((tl, hb * HEAD_DV), lambda i, h, t: (i * nt + t, h))
    grid = (b, nhb, nt)
    w32_spec, w16_spec = _wcast_specs(layer, grid, w_out.shape[1], w_out.shape[2])
    return pl.pallas_call(
        _mlstm_prompt_body,
        grid=grid,
        in_specs=[
            dk_spec(0), dk_spec(nkb), dv_spec, dv_spec, dv_spec,
            pl.BlockSpec((1, hb, 1, tl), lambda i, h, t: (i, h, 0, t)),
            pl.BlockSpec((1, hb, 1, tl), lambda i, h, t: (i, nhb + h, 0, t)),
            pl.BlockSpec((hb, 1, 1), lambda i, h, t: (h, 0, 0)),
            pl.BlockSpec((hb, 1, 1), lambda i, h, t: (nhb + h, 0, 0)),
            pl.BlockSpec((1, hb * HEAD_DV), lambda i, h, t: (0, h)),
            w32_spec,
        ],
        out_specs=[
            dv_spec,
            pl.BlockSpec((1, hb, HEAD_DK, HEAD_DV), lambda i, h, t: (i, h, 0, 0)),
            pl.BlockSpec((1, hb, 1, HEAD_DK), lambda i, h, t: (i, h, 0, 0)),
            pl.BlockSpec((1, hb, 1, 128), lambda i, h, t: (i, h, 0, 0)),
            w16_spec,
        ],
        out_shape=[jax.ShapeDtypeStruct((b * l, MAIN_DV), BF16),
                   jax.ShapeDtypeStruct((b, N_HEADS, HEAD_DK, HEAD_DV), F32),
                   jax.ShapeDtypeStruct((b, N_HEADS, 1, HEAD_DK), F32),
                   jax.ShapeDtypeStruct((b, N_HEADS, 1, 128), F32),
                   jax.ShapeDtypeStruct(w_out.shape[1:], BF16)],
        compiler_params=_cparams(("arbitrary", "arbitrary", "arbitrary")),
        name="mlstm_prompt",
    )(qk, qk, v, og, z, gates_t, gates_t, bias3, bias3, g_head.reshape(1, MAIN_DV), w_out)


def _memattn_prompt_body(q_ref, k_ref, v_ref, z_ref, u_ref, kb_ref, vb_ref):
    @pl.when(pl.program_id(2) == 0)
    def _():
        kb_ref[...] = k_ref[...].astype(BF16)
        vb_ref[...] = v_ref[...].astype(BF16)

    q = (q_ref[...] * MEM_HEAD_DIM ** -0.5).astype(BF16)
    s = lax.dot_general(q, kb_ref[...], _NT, preferred_element_type=F32)
    p = jnp.exp(s - jnp.max(s, axis=-1, keepdims=True))
    p = p * (1.0 / jnp.sum(p, axis=-1, keepdims=True))
    o = _dot(p.astype(BF16), vb_ref[...])
    z = z_ref[...]
    u_ref[...] = (o * (z * _sigmoid(z))).astype(u_ref.dtype)


def _memattn_prompt(qm, mk, mv, z, b, l):
    tl = MEMATTN_TILE_L
    nt = l // tl
    zoff = MAIN_DV // MEM_HEAD_DIM
    return pl.pallas_call(
        _memattn_prompt_body,
        grid=(b, N_MEM_HEADS, nt),
        in_specs=[
            pl.BlockSpec((tl, MEM_HEAD_DIM), lambda i, h, t: (i * nt + t, h)),
            pl.BlockSpec((N_MEM, MEM_HEAD_DIM), lambda i, h, t: (i, h)),
            pl.BlockSpec((N_MEM, MEM_HEAD_DIM), lambda i, h, t: (i, h)),
            pl.BlockSpec((tl, MEM_HEAD_DIM), lambda i, h, t: (i * nt + t, zoff + h)),
        ],
        out_specs=pl.BlockSpec((tl, MEM_HEAD_DIM), lambda i, h, t: (i * nt + t, h)),
        out_shape=jax.ShapeDtypeStruct((b * l, MEM_DIM), BF16),
        scratch_shapes=[pltpu.VMEM((N_MEM, MEM_HEAD_DIM), BF16),
                        pltpu.VMEM((N_MEM, MEM_HEAD_DIM), BF16)],
        compiler_params=_cparams(("parallel", "parallel", "arbitrary")),
        name="memattn_prompt",
    )(qm, mk, mv, z)


def _col_bcast(row16, width):
    ones = jnp.ones((BF16_SUBLANES, width), BF16)
    return lax.dot_general(row16, ones, _TN, preferred_element_type=F32)


def _first_row16(x):
    n = x.shape[1]
    r = lax.broadcasted_iota(jnp.int32, (BF16_SUBLANES, n), 0)
    return jnp.where(r == 0, jnp.broadcast_to(x, (BF16_SUBLANES, n)), 0.0).astype(BF16)


def _decode_body(a_ref, w_ref, e_ref, qk_ref, v_ref, z_ref, g_ref, s_ref, *rest,
                 use_rope, is_mlstm):
    if use_rope:
        cos_ref, sin_ref = rest[:2]
        rest = rest[2:]
    if is_mlstm:
        og_ref, n_ref, u_ref, so_ref, no_ref = rest
    else:
        u_ref, so_ref = rest
    base = pl.program_id(0) * N_HEADS
    for h in range(N_HEADS):
        a = a_ref[base + h]
        w = w_ref[base + h]
        qs = slice(h * HEAD_DK, (h + 1) * HEAD_DK)
        ks = slice(MAIN_DK + h * HEAD_DK, MAIN_DK + (h + 1) * HEAD_DK)
        vs = slice(h * HEAD_DV, (h + 1) * HEAD_DV)
        q = qk_ref[0, :, qs]
        k = qk_ref[0, :, ks]
        if use_rope:
            q = _rope(q, cos_ref[...], sin_ref[...])
            k = _rope(k, cos_ref[...], sin_ref[...])
        k = k * HEAD_DK ** -0.5
        v = v_ref[0, :, vs]
        vf = v.astype(F32)
        qb = q.astype(BF16)
        kb = k.astype(BF16)
        s = s_ref[0, h]

        qk = jnp.sum(qb.astype(F32) * kb.astype(F32), axis=1, keepdims=True)
        sc = qk * w
        q_cols = _col_bcast(_first_row16(q), HEAD_DV)
        qs_ = jnp.sum(q_cols * s, axis=0, keepdims=True)
        num = sc.astype(BF16).astype(F32) * vf + a * qs_
        kv = lax.dot_general(_first_row16(k * w), jnp.broadcast_to(v, (BF16_SUBLANES, HEAD_DV)),
                             _TN, preferred_element_type=F32)
        so_ref[0, h] = a * s + kv

        if is_mlstm:
            nv = n_ref[0, h]
            qn = jnp.sum(qb.astype(F32) * nv.astype(BF16).astype(F32), axis=1, keepdims=True)
            den = sc + a * qn
            main = num / jnp.maximum(jnp.abs(den), e_ref[base + h])
            wb = (jnp.zeros((1, 1), F32) + w).astype(BF16).astype(F32)
            no_ref[0, h] = a * nv + wb * kb.astype(F32)
            main = main * _sigmoid(og_ref[0, :, vs])
        else:
            main = num
        u_ref[0, :, vs] = _head_norm_gate(main, g_ref[:, vs], z_ref[0, :, vs]).astype(u_ref.dtype)


def _decode(a, w, e, qk, v, z, g_head, state, rope_tabs=None, og=None, n_state=None):
    b = qk.shape[0]
    is_mlstm = og is not None
    use_rope = rope_tabs is not None
    qk_spec = pl.BlockSpec((1, 1, 2 * MAIN_DK), lambda i, *_: (i, 0, 0))
    dv_spec = pl.BlockSpec((1, 1, MAIN_DV), lambda i, *_: (i, 0, 0))
    st_spec = pl.BlockSpec((1, N_HEADS, HEAD_DK, HEAD_DV), lambda i, *_: (i, 0, 0, 0))
    n_spec = pl.BlockSpec((1, N_HEADS, 1, HEAD_DK), lambda i, *_: (i, 0, 0, 0))
    in_specs = [qk_spec, dv_spec, dv_spec,
                pl.BlockSpec((1, MAIN_DV), lambda i, *_: (0, 0)), st_spec]
    args = [qk, v, z, g_head.reshape(1, MAIN_DV), state]
    out_specs = [dv_spec, st_spec]
    out_shape = [jax.ShapeDtypeStruct((b, 1, MAIN_DV), BF16),
                 jax.ShapeDtypeStruct(state.shape, F32)]
    if use_rope:
        tab = pl.BlockSpec((1, HEAD_DK // 2), lambda i, *_: (0, 0))
        in_specs += [tab, tab]
        args += list(rope_tabs)
    if is_mlstm:
        in_specs += [dv_spec, n_spec]
        args += [og, n_state]
        out_specs.append(n_spec)
        out_shape.append(jax.ShapeDtypeStruct(n_state.shape, F32))
    return pl.pallas_call(
        functools.partial(_decode_body, use_rope=use_rope, is_mlstm=is_mlstm),
        grid_spec=pltpu.PrefetchScalarGridSpec(
            num_scalar_prefetch=3, grid=(b,),
            in_specs=in_specs, out_specs=out_specs),
        out_shape=out_shape,
        compiler_params=_cparams(("parallel",)),
        name="mlstm_decode" if is_mlstm else "ret_decode",
    )(a, w, e, *args)


def _gate_decode_body(gi_ref, gf_ref, bi_ref, bf_ref, m_ref, mo_ref, a_ref, w_ref, e_ref):
    ic = gi_ref[...] + bi_ref[...]
    lf = _log_sigmoid(gf_ref[...] + bf_ref[...])
    log_prev = lf + m_ref[...]
    m_t = jnp.maximum(log_prev, ic)
    mo_ref[...] = m_t
    a_ref[...] = jnp.exp(log_prev - m_t)
    w_ref[...] = jnp.exp(ic - m_t)
    e_ref[...] = jnp.exp(-m_t)


def _gate_decode(g_i, g_f, b_i, b_f, m):
    sds = jax.ShapeDtypeStruct(m.shape, F32)
    return pl.pallas_call(
        _gate_decode_body, out_shape=[sds, sds, sds, sds], name="mlstm_decode_gates",
    )(g_i, g_f, b_i.reshape(1, N_HEADS), b_f.reshape(1, N_HEADS), m)


def _lane_tiled(x):
    lead = x.shape[:-2]
    nh, d = x.shape[-2:]
    x = x.reshape(lead + (nh, d // 128, 128))
    x = jnp.swapaxes(x, -3, -2)
    return x.reshape(lead + (d // 128 * nh, 128))


def _memattn_decode_body(q_ref, k_ref, v_ref, z_ref, u_ref):
    rows = MEM_HEAD_DIM // 128 * N_MEM_HEADS
    for bi in range(MEMDEC_ROWS):
        q = q_ref[bi] * MEM_HEAD_DIM ** -0.5
        prod = k_ref[bi] * q[None]
        part = prod[:, :rows // 2, :] + prod[:, rows // 2:, :]
        part = part + pltpu.roll(part, N_MEM_HEADS, 1)
        s = jnp.sum(part, axis=-1, keepdims=True)
        p = jnp.exp(s - jnp.max(s, axis=0, keepdims=True))
        p = p / jnp.sum(p, axis=0, keepdims=True)
        o = jnp.concatenate([jnp.sum(p * v_ref[bi, :, :rows // 2, :], axis=0),
                             jnp.sum(p * v_ref[bi, :, rows // 2:, :], axis=0)], axis=0)
        z = z_ref[bi]
        u_ref[bi] = (o * (z * _sigmoid(z))).astype(u_ref.dtype)


def _memattn_decode(qm, cache_k, cache_v, z, layer):
    b = qm.shape[0]
    nb = MEMDEC_ROWS
    rows = qm.shape[1]
    row = pl.BlockSpec((nb, rows, 128), lambda i: (i, 0, 0))
    kv = pl.BlockSpec((None, nb, N_MEM, rows, 128), lambda i: (layer, i, 0, 0, 0))
    return pl.pallas_call(
        _memattn_decode_body,
        grid=(b // nb,),
        in_specs=[row, kv, kv, row],
        out_specs=row,
        out_shape=jax.ShapeDtypeStruct((b, rows, 128), BF16),
        compiler_params=_cparams(("parallel",)),
        name=f"memattn_decode_{layer}",
    )(qm, cache_k, cache_v, z)


def _from_lane_tiled(x, nh):
    lead = x.shape[:-2]
    nt = x.shape[-2] // nh
    x = x.reshape(lead + (nt, nh, 128))
    x = jnp.swapaxes(x, -3, -2)
    return x.reshape(lead + (nh, nt * 128))


def _rope_tables(pos):
    d = HEAD_DK
    inv = 1.0 / (ROPE_THETA ** (jnp.arange(0, d, 2, dtype=F32) / d))
    ang = pos[:, None] * inv[None, :]
    return jnp.cos(ang), jnp.sin(ang)


_REGIONS = (("qk", COL_QK, COL_V, F32), ("v", COL_V, COL_QM, BF16), ("qm", COL_QM, COL_Z, F32),
            ("z", COL_Z, COL_OG, F32), ("og", COL_OG, COL_GATES, F32))


def _in_proj(h, w, tag, with_mlstm):
    w_is_nk = with_mlstm
    wk = jnp.swapaxes(w, 1, 2) if w_is_nk else w
    out = {}
    for name, c0, c1, dt in _REGIONS:
        if name == "og" and not with_mlstm:
            continue
        out[name] = _matmul_w32(h, wk, 0, c0, c1 - c0, dt, f"inproj_{name}_{tag}", w_is_nk)
    if with_mlstm:
        wg = jnp.pad(w[0, :, COL_GATES:], ((0, 0), (0, 128 - 2 * N_HEADS))).astype(BF16)
        out["gates"] = _matmul(h, wg, F32, f"inproj_gates_{tag}")
    return out


def kernel(x_prompt, x_sample, mem_prompt, state_ret, state_mlstm_C, state_mlstm_n, state_mlstm_m,
           cache_mem_k, cache_mem_v, g_pre, g_post, w_in_ret, w_in_mlstm, b_gate_mlstm, g_head,
           w_out, g_mem, w_mem_kv):
    bp, lp, _ = x_prompt.shape
    bs = x_sample.shape[0]
    mp = bp * lp

    mem2d = mem_prompt.reshape(bp * N_MEM, D_MODEL)
    mem_k, mem_v = [], []
    for l in range(DEPTH):
        hm = _rmsnorm(mem2d, g_mem[l], f"mem_norm_{l}")
        mem_k.append(_matmul_w32(hm, w_mem_kv, l, 0, MEM_DIM, F32, f"mem_k_{l}"))
        mem_v.append(_matmul_w32(hm, w_mem_kv, l, MEM_DIM, MEM_DIM, F32, f"mem_v_{l}"))

    cos_p, sin_p = _rope_tables(jnp.arange(lp, dtype=F32))
    cos_s, sin_s = _rope_tables(PAST_LEN + jnp.arange(1, dtype=F32))
    bias = b_gate_mlstm[0]
    xp = x_prompt.reshape(mp, D_MODEL)
    xs = x_sample.reshape(bs, D_MODEL)
    srow = lambda a: a[mp:].reshape(bs, 1, a.shape[-1])
    smem = lambda a: _lane_tiled(a[mp:].reshape(bs, N_MEM_HEADS, MEM_HEAD_DIM))
    cache_k = _lane_tiled(cache_mem_k)
    cache_v = _lane_tiled(cache_mem_v)
    unmem = lambda u: _from_lane_tiled(u, N_MEM_HEADS).reshape(bs, MEM_DIM)

    h = _rmsnorm_stacked(xp, xs, g_pre[0])
    pr = _in_proj(h, w_in_ret, "0", False)
    u_main_p, ret_prompt, w_o = _ret_prompt(pr["qk"], pr["v"], pr["z"], cos_p, sin_p, g_head[0],
                                            w_out, 0, bp, lp)
    u_mem_p = _memattn_prompt(pr["qm"], mem_k[0], mem_v[0], pr["z"], bp, lp)
    out_p = _outproj(u_main_p, u_mem_p, w_o, "outproj_p0")

    gamma = 1.0 - 2.0 ** (-5.0 - jnp.arange(N_HEADS, dtype=F32))
    a_ret = jnp.exp(jnp.log(gamma))
    a_ret = jnp.broadcast_to(a_ret[None, :], (bs, N_HEADS)).reshape(-1)
    ones = jnp.ones((bs * N_HEADS,), F32)
    u_main_s, ret_sample = _decode(a_ret, ones, ones, srow(pr["qk"]), srow(pr["v"]), srow(pr["z"]),
                                   g_head[0], state_ret[0], rope_tabs=(cos_s, sin_s))
    u_mem_s = _memattn_decode(smem(pr["qm"]), cache_k, cache_v, smem(pr["z"][:, MAIN_DV:]), 0)
    out_s = _outproj(u_main_s.reshape(bs, MAIN_DV), unmem(u_mem_s), w_o, "outproj_s0")
    x1, h = _post_mid(out_p, out_s, xp, xs, g_post[0], g_pre[1])

    pr = _in_proj(h, w_in_mlstm, "1", True)
    gates = pr["gates"][:, :2 * N_HEADS]
    gates_t = gates[:mp].reshape(bp, lp, 2 * N_HEADS).transpose(0, 2, 1)[:, :, None, :]
    u_main_p, c_prompt, n_prompt, m_prompt, w_o = _mlstm_prompt(
        pr["qk"], pr["v"], pr["og"], pr["z"], gates_t, bias, g_head[1], w_out, 1, bp, lp)
    u_mem_p = _memattn_prompt(pr["qm"], mem_k[1], mem_v[1], pr["z"], bp, lp)
    out_p = _outproj(u_main_p, u_mem_p, w_o, "outproj_p1")
    y_prompt = _post_last(out_p, x1, 0, g_post[1], "post_p1")

    g_i = gates[mp:, :N_HEADS]
    g_f = gates[mp:, N_HEADS:]
    m_new, a_m, w_m, e_m = _gate_decode(g_i, g_f, bias[:N_HEADS], bias[N_HEADS:], state_mlstm_m[0])
    u_main_s, c_sample, n_sample = _decode(
        a_m.reshape(-1), w_m.reshape(-1), e_m.reshape(-1), srow(pr["qk"]), srow(pr["v"]),
        srow(pr["z"]), g_head[1], state_mlstm_C[0], og=srow(pr["og"]),
        n_state=state_mlstm_n[0].reshape(bs, N_HEADS, 1, HEAD_DK))
    u_mem_s = _memattn_decode(smem(pr["qm"]), cache_k, cache_v, smem(pr["z"][:, MAIN_DV:]), 1)
    out_s = _outproj(u_main_s.reshape(bs, MAIN_DV), unmem(u_mem_s), w_o, "outproj_s1")
    y_sample = _post_last(out_s, x1, mp, g_post[1], "post_s1")

    shape_kv = (DEPTH, bp, N_MEM, N_MEM_HEADS, MEM_HEAD_DIM)
    return (
        y_prompt.reshape(bp, lp, D_MODEL),
        y_sample.reshape(bs, 1, D_MODEL),
        ret_prompt[None],
        c_prompt[None],
        n_prompt.reshape(1, bp, N_HEADS, HEAD_DK),
        m_prompt[:, :, 0, 0][None],
        jnp.stack(mem_k).reshape(shape_kv),
        jnp.stack(mem_v).reshape(shape_kv),
        ret_sample[None],
        c_sample[None],
        n_sample.reshape(1, bs, N_HEADS, HEAD_DK),
        m_new[None],
    )
```

```python
import functools

import jax
import jax.numpy as jnp
from jax import lax
from jax.experimental import pallas as pl
from jax.experimental.pallas import tpu as pltpu

F32 = jnp.float32
BF16 = jnp.bfloat16

D_MODEL = 4096
DEPTH = 2
PAST_LEN = 16384
MIX_WIDTH = 2 * D_MODEL
N_HEADS = 12
HEAD_DV = MIX_WIDTH // 16
HEAD_DK = HEAD_DV // 2
MAIN_DK = N_HEADS * HEAD_DK
MAIN_DV = N_HEADS * HEAD_DV
N_MEM = 256
N_MEM_HEADS = 4
MEM_HEAD_DIM = MIX_WIDTH // 16
MEM_DIM = N_MEM_HEADS * MEM_HEAD_DIM
CHUNK = 128
ROPE_THETA = 10000.0
EPS = 1e-6
NEG_INF = -1e30

COL_QK = 0
COL_V = 2 * MAIN_DK
COL_QM = COL_V + MAIN_DV
COL_Z = COL_QM + MEM_DIM
COL_OG = COL_Z + MIX_WIDTH
COL_GATES = COL_OG + MAIN_DV

VMEM_LIMIT_BYTES = 56 * 1024 * 1024
BF16_SUBLANES = 16
MM_MAX_TILE_M = 640
MM_TILE_N = 1024
MM_VMEM_LIMIT_BYTES = 60 * 1024 * 1024
OUT_TILE_M = 1024
OUT_TILE_N = 512
SCAN_TILE_L = 512
RET_SCAN_HEADS = 4
MLSTM_SCAN_HEADS = 2
MEMATTN_TILE_L = 2048
ROW_TILE = 128
POST_TILE = 256
MEMDEC_ROWS = 4

_NT = (((1,), (1,)), ((), ()))
_TN = (((0,), (0,)), ((), ()))


def _cparams(sem, vmem_limit_bytes=VMEM_LIMIT_BYTES):
    return pltpu.CompilerParams(dimension_semantics=sem, vmem_limit_bytes=vmem_limit_bytes)


def _row_tile(m, limit):
    best = None
    for t in range(BF16_SUBLANES, limit + 1, BF16_SUBLANES):
        if m % t == 0:
            best = t
    assert best is not None, m
    return best


def _sigmoid(x):
    return 0.5 * jnp.tanh(0.5 * x) + 0.5


def _log_sigmoid(x):
    return jnp.minimum(x, 0.0) - jnp.log(1.0 + jnp.exp(-jnp.abs(x)))


def _rope(x, cos, sin):
    half = HEAD_DK // 2
    x1, x2 = x[:, :half], x[:, half:]
    return jnp.concatenate([x1 * cos - x2 * sin, x1 * sin + x2 * cos], axis=-1)


def _head_norm_gate(main, g, z):
    mu = jnp.mean(main, axis=-1, keepdims=True)
    d = main - mu
    var = jnp.mean(d * d, axis=-1, keepdims=True)
    hn = d * lax.rsqrt(var + EPS)
    return hn * g * (z * _sigmoid(z))


def _rms(x, g):
    ms = jnp.mean(x * x, axis=-1, keepdims=True)
    return x * lax.rsqrt(ms + EPS) * g


def _dot(a, b):
    return jnp.dot(a, b, preferred_element_type=F32)


def _rmsnorm_body(x_ref, g_ref, o_ref):
    o_ref[...] = _rms(x_ref[...], g_ref[...]).astype(o_ref.dtype)


def _rmsnorm(x, g, name):
    m, d = x.shape
    tm = ROW_TILE
    return pl.pallas_call(
        _rmsnorm_body,
        grid=(m // tm,),
        in_specs=[pl.BlockSpec((tm, d), lambda i: (i, 0)),
                  pl.BlockSpec((1, d), lambda i: (0, 0))],
        out_specs=pl.BlockSpec((tm, d), lambda i: (i, 0)),
        out_shape=jax.ShapeDtypeStruct((m, d), BF16),
        compiler_params=_cparams(("parallel",)),
        name=name,
    )(x, g.reshape(1, d))


def _rmsnorm_stacked_body(xp_ref, xs_ref, g_ref, o_ref, *, n_prompt_tiles):
    i = pl.program_id(0)

    @pl.when(i < n_prompt_tiles)
    def _():
        o_ref[...] = _rms(xp_ref[...], g_ref[...]).astype(o_ref.dtype)

    @pl.when(i >= n_prompt_tiles)
    def _():
        o_ref[...] = _rms(xs_ref[...], g_ref[...]).astype(o_ref.dtype)


def _rmsnorm_stacked(xp, xs, g):
    mp, d = xp.shape
    ms = xs.shape[0]
    tm = ROW_TILE
    npt, nst = mp // tm, ms // tm
    return pl.pallas_call(
        functools.partial(_rmsnorm_stacked_body, n_prompt_tiles=npt),
        grid=(npt + nst,),
        in_specs=[pl.BlockSpec((tm, d), lambda i: (jnp.minimum(i, npt - 1), 0)),
                  pl.BlockSpec((tm, d), lambda i: (jnp.maximum(i - npt, 0), 0)),
                  pl.BlockSpec((1, d), lambda i: (0, 0))],
        out_specs=pl.BlockSpec((tm, d), lambda i: (i, 0)),
        out_shape=jax.ShapeDtypeStruct((mp + ms, d), BF16),
        compiler_params=_cparams(("arbitrary",)),
        name="pre_norm_0",
    )(xp, xs, g.reshape(1, d))


def _mm_w32_body(a_ref, w_ref, o_ref, wb_ref, *, w_is_nk):
    @pl.when(pl.program_id(1) == 0)
    def _():
        wb_ref[...] = w_ref[...].astype(BF16)

    if w_is_nk:
        acc = lax.dot_general(a_ref[...], wb_ref[...], _NT, preferred_element_type=F32)
    else:
        acc = _dot(a_ref[...], wb_ref[...])
    o_ref[...] = acc.astype(o_ref.dtype)


def _matmul_w32(a, w, layer, col0, ncols, out_dtype, name, w_is_nk=False):
    m, k = a.shape
    tm = _row_tile(m, MM_MAX_TILE_M)
    tn = min(MM_TILE_N, ncols)
    assert col0 % tn == 0 and ncols % tn == 0
    off = col0 // tn
    if w_is_nk:
        w_spec = pl.BlockSpec((None, tn, k), lambda j, i: (layer, off + j, 0))
        wb_shape = (tn, k)
    else:
        w_spec = pl.BlockSpec((None, k, tn), lambda j, i: (layer, 0, off + j))
        wb_shape = (k, tn)
    return pl.pallas_call(
        functools.partial(_mm_w32_body, w_is_nk=w_is_nk),
        grid=(ncols // tn, m // tm),
        in_specs=[pl.BlockSpec((tm, k), lambda j, i: (i, 0)), w_spec],
        out_specs=pl.BlockSpec((tm, tn), lambda j, i: (i, j)),
        out_shape=jax.ShapeDtypeStruct((m, ncols), out_dtype),
        scratch_shapes=[pltpu.VMEM(wb_shape, BF16)],
        compiler_params=_cparams(("parallel", "arbitrary"), MM_VMEM_LIMIT_BYTES),
        name=name,
    )(a, w)


def _mm_body(a_ref, w_ref, o_ref):
    o_ref[...] = _dot(a_ref[...], w_ref[...]).astype(o_ref.dtype)


def _matmul(a, w, out_dtype, name):
    m, k = a.shape
    n = w.shape[1]
    tm = _row_tile(m, MM_MAX_TILE_M)
    return pl.pallas_call(
        _mm_body,
        grid=(m // tm,),
        in_specs=[pl.BlockSpec((tm, k), lambda i: (i, 0)),
                  pl.BlockSpec((k, n), lambda i: (0, 0))],
        out_specs=pl.BlockSpec((tm, n), lambda i: (i, 0)),
        out_shape=jax.ShapeDtypeStruct((m, n), out_dtype),
        compiler_params=_cparams(("parallel",)),
        name=name,
    )(a, w)


def _outproj_body(um_ref, ue_ref, w_ref, o_ref):
    o_ref[...] = (_dot(um_ref[...], w_ref[:MAIN_DV, :]) + _dot(ue_ref[...], w_ref[MAIN_DV:, :]))


def _outproj(u_main, u_mem, w, name):
    m = u_main.shape[0]
    n = w.shape[1]
    tm = _row_tile(m, OUT_TILE_M)
    tn = OUT_TILE_N
    return pl.pallas_call(
        _outproj_body,
        grid=(m // tm, n // tn),
        in_specs=[pl.BlockSpec((tm, MAIN_DV), lambda i, j: (i, 0)),
                  pl.BlockSpec((tm, MEM_DIM), lambda i, j: (i, 0)),
                  pl.BlockSpec((MIX_WIDTH, tn), lambda i, j: (0, j))],
        out_specs=pl.BlockSpec((tm, tn), lambda i, j: (i, j)),
        out_shape=jax.ShapeDtypeStruct((m, n), F32),
        compiler_params=_cparams(("parallel", "parallel")),
        name=name,
    )(u_main, u_mem, w)


def _post_mid_body(op_ref, os_ref, xp_ref, xs_ref, gp_ref, gn_ref, y_ref, h_ref, *, n_prompt_tiles):
    i = pl.program_id(0)

    def emit(o_ref, x_ref):
        y = x_ref[...] + _rms(o_ref[...], gp_ref[...])
        y_ref[...] = y
        h_ref[...] = _rms(y, gn_ref[...]).astype(h_ref.dtype)

    @pl.when(i < n_prompt_tiles)
    def _():
        emit(op_ref, xp_ref)

    @pl.when(i >= n_prompt_tiles)
    def _():
        emit(os_ref, xs_ref)


def _post_mid(out_p, out_s, x_p, x_s, g_post, g_next):
    mp, d = x_p.shape
    ms = x_s.shape[0]
    tm = ROW_TILE
    npt, nst = mp // tm, ms // tm
    prow = pl.BlockSpec((tm, d), lambda i: (jnp.minimum(i, npt - 1), 0))
    srow = pl.BlockSpec((tm, d), lambda i: (jnp.maximum(i - npt, 0), 0))
    vec = pl.BlockSpec((1, d), lambda i: (0, 0))
    orow = pl.BlockSpec((tm, d), lambda i: (i, 0))
    return pl.pallas_call(
        functools.partial(_post_mid_body, n_prompt_tiles=npt),
        grid=(npt + nst,),
        in_specs=[prow, srow, prow, srow, vec, vec],
        out_specs=[orow, orow],
        out_shape=[jax.ShapeDtypeStruct((mp + ms, d), F32),
                   jax.ShapeDtypeStruct((mp + ms, d), BF16)],
        compiler_params=_cparams(("arbitrary",)),
        name="post_0",
    )(out_p, out_s, x_p, x_s, g_post.reshape(1, d), g_next.reshape(1, d))


def _post_last_body(o_ref, x_ref, gp_ref, y_ref):
    y_ref[...] = x_ref[...] + _rms(o_ref[...], gp_ref[...])


def _post_last(out, x_all, row0, g_post, name):
    m, d = out.shape
    tm = min(POST_TILE, m)
    assert row0 % tm == 0
    off = row0 // tm
    return pl.pallas_call(
        _post_last_body, grid=(m // tm,),
        in_specs=[pl.BlockSpec((tm, d), lambda i: (i, 0)),
                  pl.BlockSpec((tm, d), lambda i: (off + i, 0)),
                  pl.BlockSpec((1, d), lambda i: (0, 0))],
        out_specs=pl.BlockSpec((tm, d), lambda i: (i, 0)),
        out_shape=jax.ShapeDtypeStruct((m, d), F32),
        compiler_params=_cparams(("parallel",)), name=name,
    )(out, x_all, g_post.reshape(1, d))


def _wcast_blocks(n_steps):
    nblk = 1
    while nblk * 2 <= n_steps:
        nblk *= 2
    return nblk


def _wcast_specs(layer, grid, n_rows, n_cols):
    n_steps = grid[0] * grid[1] * grid[2]
    nblk = _wcast_blocks(n_steps)
    rows = n_rows // nblk
    assert rows * nblk == n_rows and rows % BF16_SUBLANES == 0

    def blk(i, h, t):
        return jnp.minimum((i * grid[1] + h) * grid[2] + t, nblk - 1)

    return (pl.BlockSpec((None, rows, n_cols), lambda i, h, t: (layer, blk(i, h, t), 0)),
            pl.BlockSpec((rows, n_cols), lambda i, h, t: (blk(i, h, t), 0)))


def _ret_prompt_body(q_ref, k_ref, v_ref, z_ref, cos_ref, sin_ref, din_ref, qd_ref, kd_ref,
                     cd_ref, g_ref, w32_ref, u_ref, s_ref, w16_ref):
    w16_ref[...] = w32_ref[...].astype(BF16)

    @pl.when(pl.program_id(2) == 0)
    def _():
        s_ref[...] = jnp.zeros_like(s_ref)

    for ci in range(SCAN_TILE_L // CHUNK):
        sl = slice(ci * CHUNK, (ci + 1) * CHUNK)
        cos, sin = cos_ref[sl, :], sin_ref[sl, :]
        for hh in range(RET_SCAN_HEADS):
            ks = slice(hh * HEAD_DK, (hh + 1) * HEAD_DK)
            vs = slice(hh * HEAD_DV, (hh + 1) * HEAD_DV)
            din = din_ref[hh]
            qd = qd_ref[hh]
            kd = kd_ref[hh]
            cd = cd_ref[hh][:, :1]
            q = _rope(q_ref[sl, ks], cos, sin)
            k = _rope(k_ref[sl, ks], cos, sin) * HEAD_DK ** -0.5
            v = v_ref[sl, vs]
            s = s_ref[0, hh]
            sc = lax.dot_general(q.astype(BF16), k.astype(BF16), _NT,
                                 preferred_element_type=F32) * din
            o = _dot(sc.astype(BF16), v) + _dot((q * qd).astype(BF16), s.astype(BF16))
            s_ref[0, hh] = s * cd + lax.dot_general((k * kd).astype(BF16), v, _TN,
                                                    preferred_element_type=F32)
            u_ref[sl, vs] = _head_norm_gate(o, g_ref[:, vs], z_ref[sl, vs]).astype(u_ref.dtype)


def _ret_prompt(qk, v, z, cos, sin, g_head, w_out, layer, b, l):
    tl, c, hb = SCAN_TILE_L, CHUNK, RET_SCAN_HEADS
    nt = l // tl
    grid = (b, N_HEADS // hb, nt)
    w32_spec, w16_spec = _wcast_specs(layer, grid, w_out.shape[1], w_out.shape[2])
    lg = jnp.log(1.0 - 2.0 ** (-5.0 - jnp.arange(N_HEADS, dtype=F32)))
    idx = jnp.arange(c, dtype=F32)
    diff = idx[:, None] - idx[None, :]
    decay_in = jnp.where(diff[None] >= 0.0,
                         jnp.exp(lg[:, None, None] * jnp.maximum(diff, 0.0)[None]), 0.0)
    q_dec = jnp.exp(lg[:, None] * (idx + 1.0)[None, :])[:, :, None]
    k_dec = jnp.exp(lg[:, None] * (c - 1.0 - idx)[None, :])[:, :, None]
    chunk_dec = jnp.broadcast_to(jnp.exp(lg * c)[:, None, None], (N_HEADS, 1, 128))
    nkb = MAIN_DK // (hb * HEAD_DK)
    return pl.pallas_call(
        _ret_prompt_body,
        grid=grid,
        in_specs=[
            pl.BlockSpec((tl, hb * HEAD_DK), lambda i, h, t: (i * nt + t, h)),
            pl.BlockSpec((tl, hb * HEAD_DK), lambda i, h, t: (i * nt + t, nkb + h)),
            pl.BlockSpec((tl, hb * HEAD_DV), lambda i, h, t: (i * nt + t, h)),
            pl.BlockSpec((tl, hb * HEAD_DV), lambda i, h, t: (i * nt + t, h)),
            pl.BlockSpec((tl, HEAD_DK // 2), lambda i, h, t: (t, 0)),
            pl.BlockSpec((tl, HEAD_DK // 2), lambda i, h, t: (t, 0)),
            pl.BlockSpec((hb, c, c), lambda i, h, t: (h, 0, 0)),
            pl.BlockSpec((hb, c, 1), lambda i, h, t: (h, 0, 0)),
            pl.BlockSpec((hb, c, 1), lambda i, h, t: (h, 0, 0)),
            pl.BlockSpec((hb, 1, 128), lambda i, h, t: (h, 0, 0)),
            pl.BlockSpec((1, hb * HEAD_DV), lambda i, h, t: (0, h)),
            w32_spec,
        ],
        out_specs=[
            pl.BlockSpec((tl, hb * HEAD_DV), lambda i, h, t: (i * nt + t, h)),
            pl.BlockSpec((1, hb, HEAD_DK, HEAD_DV), lambda i, h, t: (i, h, 0, 0)),
            w16_spec,
        ],
        out_shape=[jax.ShapeDtypeStruct((b * l, MAIN_DV), BF16),
                   jax.ShapeDtypeStruct((b, N_HEADS, HEAD_DK, HEAD_DV), F32),
                   jax.ShapeDtypeStruct(w_out.shape[1:], BF16)],
        compiler_params=_cparams(("arbitrary", "arbitrary", "arbitrary")),
        name="ret_prompt",
    )(qk, qk, v, z, cos, sin, decay_in, q_dec, k_dec, chunk_dec, g_head.reshape(1, MAIN_DV),
      w_out)


def _col_from_row(row, eye):
    return jnp.sum(jnp.where(eye, row, 0.0), axis=1, keepdims=True)


def _mlstm_prompt_body(q_ref, k_ref, v_ref, og_ref, z_ref, gi_ref, gf_ref, bi_ref, bf_ref, g_ref,
                       w32_ref, u_ref, c_ref, n_ref, m_ref, w16_ref):
    w16_ref[...] = w32_ref[...].astype(BF16)

    @pl.when(pl.program_id(2) == 0)
    def _():
        c_ref[...] = jnp.zeros_like(c_ref)
        n_ref[...] = jnp.zeros_like(n_ref)
        m_ref[...] = jnp.full_like(m_ref, NEG_INF)

    c = CHUNK
    ri = lax.broadcasted_iota(jnp.int32, (c, c), 0)
    cj = lax.broadcasted_iota(jnp.int32, (c, c), 1)
    eye = ri == cj
    causal = ri >= cj
    lane8 = lax.broadcasted_iota(jnp.int32, (8, c), 1)
    for ci in range(SCAN_TILE_L // CHUNK):
        sl = slice(ci * CHUNK, (ci + 1) * CHUNK)
        for hh in range(MLSTM_SCAN_HEADS):
            ks = slice(hh * HEAD_DK, (hh + 1) * HEAD_DK)
            vs = slice(hh * HEAD_DV, (hh + 1) * HEAD_DV)
            ic = gi_ref[0, hh, :, sl] + bi_ref[hh]
            lf = _log_sigmoid(gf_ref[0, hh, :, sl] + bf_ref[hh])
            cs = jnp.broadcast_to(lf, (8, c))
            sh = 1
            while sh < c:
                cs = cs + jnp.where(lane8 >= sh, pltpu.roll(cs, sh, 1), 0.0)
                sh *= 2
            b_row = cs[:1, :]
            r_row = ic - b_row
            b_col = _col_from_row(b_row, eye)
            r_col = _col_from_row(r_row, eye)
            b_last = b_row[:, c - 1:]
            m_prev = m_ref[0, hh][:, :1]

            logw = jnp.where(causal, b_col + r_row, NEG_INF)
            log_prev = b_col + m_prev
            m_t = jnp.maximum(log_prev, jnp.max(logw, axis=1, keepdims=True))
            w = jnp.exp(logw - m_t)
            a_prev = jnp.exp(log_prev - m_t)

            q = q_ref[sl, ks]
            k = k_ref[sl, ks] * HEAD_DK ** -0.5
            v = v_ref[sl, vs]
            qb, kb = q.astype(BF16), k.astype(BF16)
            cm = c_ref[0, hh]
            nv = n_ref[0, hh]
            sc = lax.dot_general(qb, kb, _NT, preferred_element_type=F32) * w
            num = _dot(sc.astype(BF16), v) + a_prev * _dot(qb, cm.astype(BF16))
            qn = jnp.sum(qb.astype(F32) * nv.astype(BF16).astype(F32), axis=1, keepdims=True)
            den = jnp.sum(sc, axis=1, keepdims=True) + a_prev * qn
            hc = num * (1.0 / jnp.maximum(jnp.abs(den), jnp.exp(-m_t)))

            m_new = m_t[c - 1:, :]
            a_c = jnp.exp(b_last + m_prev - m_new)
            wk_col = jnp.exp(b_last + r_col - m_new)
            wk_row = jnp.exp(b_last + r_row - m_new)
            c_ref[0, hh] = a_c * cm + lax.dot_general((k * wk_col).astype(BF16), v, _TN,
                                                      preferred_element_type=F32)
            wk8 = jnp.broadcast_to(wk_row, (8, c)).astype(BF16)
            n_ref[0, hh] = a_c * nv + _dot(wk8, kb)[:1, :]
            m_ref[0, hh] = jnp.broadcast_to(m_new, (1, 128))

            main = hc * _sigmoid(og_ref[sl, vs])
            u_ref[sl, vs] = _head_norm_gate(main, g_ref[:, vs], z_ref[sl, vs]).astype(u_ref.dtype)


def _mlstm_prompt(qk, v, og, z, gates_t, bias, g_head, w_out, layer, b, l):
    tl, hb = SCAN_TILE_L, MLSTM_SCAN_HEADS
    nt = l // tl
    bias3 = bias.reshape(2 * N_HEADS, 1, 1)
    nkb = MAIN_DK // (hb * HEAD_DK)
    nhb = N_HEADS // hb
    dk_spec = lambda off: pl.BlockSpec((tl, hb * HEAD_DK), lambda i, h, t: (i * nt + t, off + h))
    dv_spec = pl.BlockSpec((tl, hb * HEAD_DV), lambda i, h, t: (i * nt + t, h))
    grid = (b, nhb, nt)
    w32_spec, w16_spec = _wcast_specs(layer, grid, w_out.shape[1], w_out.shape[2])
    return pl.pallas_call(
        _mlstm_prompt_body,
        grid=grid,
        in_specs=[
            dk_spec(0), dk_spec(nkb), dv_spec, dv_spec, dv_spec,
            pl.BlockSpec((1, hb, 1, tl), lambda i, h, t: (i, h, 0, t)),
            pl.BlockSpec((1, hb, 1, tl), lambda i, h, t: (i, nhb + h, 0, t)),
            pl.BlockSpec((hb, 1, 1), lambda i, h, t: (h, 0, 0)),
            pl.BlockSpec((hb, 1, 1), lambda i, h, t: (nhb + h, 0, 0)),
            pl.BlockSpec((1, hb * HEAD_DV), lambda i, h, t: (0, h)),
            w32_spec,
        ],
        out_specs=[
            dv_spec,
            pl.BlockSpec((1, hb, HEAD_DK, HEAD_DV), lambda i, h, t: (i, h, 0, 0)),
            pl.BlockSpec((1, hb, 1, HEAD_DK), lambda i, h, t: (i, h, 0, 0)),
            pl.BlockSpec((1, hb, 1, 128), lambda i, h, t: (i, h, 0, 0)),
            w16_spec,
        ],
        out_shape=[jax.ShapeDtypeStruct((b * l, MAIN_DV), BF16),
                   jax.ShapeDtypeStruct((b, N_HEADS, HEAD_DK, HEAD_DV), F32),
                   jax.ShapeDtypeStruct((b, N_HEADS, 1, HEAD_DK), F32),
                   jax.ShapeDtypeStruct((b, N_HEADS, 1, 128), F32),
                   jax.ShapeDtypeStruct(w_out.shape[1:], BF16)],
        compiler_params=_cparams(("arbitrary", "arbitrary", "arbitrary")),
        name="mlstm_prompt",
    )(qk, qk, v, og, z, gates_t, gates_t, bias3, bias3, g_head.reshape(1, MAIN_DV), w_out)


def _memattn_prompt_body(q_ref, k_ref, v_ref, z_ref, u_ref, kb_ref, vb_ref):
    @pl.when(pl.program_id(2) == 0)
    def _():
        kb_ref[...] = k_ref[...].astype(BF16)
        vb_ref[...] = v_ref[...].astype(BF16)

    q = (q_ref[...] * MEM_HEAD_DIM ** -0.5).astype(BF16)
    s = lax.dot_general(q, kb_ref[...], _NT, preferred_element_type=F32)
    p = jnp.exp(s - jnp.max(s, axis=-1, keepdims=True))
    p = p * (1.0 / jnp.sum(p, axis=-1, keepdims=True))
    o = _dot(p.astype(BF16), vb_ref[...])
    z = z_ref[...]
    u_ref[...] = (o * (z * _sigmoid(z))).astype(u_ref.dtype)


def _memattn_prompt(qm, mkv, z, b, l):
    tl = MEMATTN_TILE_L
    nt = l // tl
    zoff = MAIN_DV // MEM_HEAD_DIM
    return pl.pallas_call(
        _memattn_prompt_body,
        grid=(b, N_MEM_HEADS, nt),
        in_specs=[
            pl.BlockSpec((tl, MEM_HEAD_DIM), lambda i, h, t: (i * nt + t, h)),
            pl.BlockSpec((N_MEM, MEM_HEAD_DIM), lambda i, h, t: (i, h)),
            pl.BlockSpec((N_MEM, MEM_HEAD_DIM), lambda i, h, t: (i, N_MEM_HEADS + h)),
            pl.BlockSpec((tl, MEM_HEAD_DIM), lambda i, h, t: (i * nt + t, zoff + h)),
        ],
        out_specs=pl.BlockSpec((tl, MEM_HEAD_DIM), lambda i, h, t: (i * nt + t, h)),
        out_shape=jax.ShapeDtypeStruct((b * l, MEM_DIM), BF16),
        scratch_shapes=[pltpu.VMEM((N_MEM, MEM_HEAD_DIM), BF16),
                        pltpu.VMEM((N_MEM, MEM_HEAD_DIM), BF16)],
        compiler_params=_cparams(("parallel", "parallel", "arbitrary")),
        name="memattn_prompt",
    )(qm, mkv, mkv, z)


def _col_bcast(row16, width):
    ones = jnp.ones((BF16_SUBLANES, width), BF16)
    return lax.dot_general(row16, ones, _TN, preferred_element_type=F32)


def _first_row16(x):
    n = x.shape[1]
    r = lax.broadcasted_iota(jnp.int32, (BF16_SUBLANES, n), 0)
    return jnp.where(r == 0, jnp.broadcast_to(x, (BF16_SUBLANES, n)), 0.0).astype(BF16)


def _decode_body(a_ref, w_ref, e_ref, qk_ref, v_ref, z_ref, g_ref, s_ref, *rest,
                 use_rope, is_mlstm):
    if use_rope:
        cos_ref, sin_ref = rest[:2]
        rest = rest[2:]
    if is_mlstm:
        og_ref, n_ref, u_ref, so_ref, no_ref = rest
    else:
        u_ref, so_ref = rest
    base = pl.program_id(0) * N_HEADS
    for h in range(N_HEADS):
        a = a_ref[base + h]
        w = w_ref[base + h]
        qs = slice(h * HEAD_DK, (h + 1) * HEAD_DK)
        ks = slice(MAIN_DK + h * HEAD_DK, MAIN_DK + (h + 1) * HEAD_DK)
        vs = slice(h * HEAD_DV, (h + 1) * HEAD_DV)
        q = qk_ref[0, :, qs]
        k = qk_ref[0, :, ks]
        if use_rope:
            q = _rope(q, cos_ref[...], sin_ref[...])
            k = _rope(k, cos_ref[...], sin_ref[...])
        k = k * HEAD_DK ** -0.5
        v = v_ref[0, :, vs]
        vf = v.astype(F32)
        qb = q.astype(BF16)
        kb = k.astype(BF16)
        s = s_ref[0, h]

        qk = jnp.sum(qb.astype(F32) * kb.astype(F32), axis=1, keepdims=True)
        sc = qk * w
        q_cols = _col_bcast(_first_row16(q), HEAD_DV)
        qs_ = jnp.sum(q_cols * s, axis=0, keepdims=True)
        num = sc.astype(BF16).astype(F32) * vf + a * qs_
        kv = lax.dot_general(_first_row16(k * w), jnp.broadcast_to(v, (BF16_SUBLANES, HEAD_DV)),
                             _TN, preferred_element_type=F32)
        so_ref[0, h] = a * s + kv

        if is_mlstm:
            nv = n_ref[0, h]
            qn = jnp.sum(qb.astype(F32) * nv.astype(BF16).astype(F32), axis=1, keepdims=True)
            den = sc + a * qn
            main = num / jnp.maximum(jnp.abs(den), e_ref[base + h])
            wb = (jnp.zeros((1, 1), F32) + w).astype(BF16).astype(F32)
            no_ref[0, h] = a * nv + wb * kb.astype(F32)
            main = main * _sigmoid(og_ref[0, :, vs])
        else:
            main = num
        u_ref[0, :, vs] = _head_norm_gate(main, g_ref[:, vs], z_ref[0, :, vs]).astype(u_ref.dtype)


def _decode(a, w, e, qk, v, z, g_head, state, rope_tabs=None, og=None, n_state=None):
    b = qk.shape[0]
    is_mlstm = og is not None
    use_rope = rope_tabs is not None
    qk_spec = pl.BlockSpec((1, 1, 2 * MAIN_DK), lambda i, *_: (i, 0, 0))
    dv_spec = pl.BlockSpec((1, 1, MAIN_DV), lambda i, *_: (i, 0, 0))
    st_spec = pl.BlockSpec((1, N_HEADS, HEAD_DK, HEAD_DV), lambda i, *_: (i, 0, 0, 0))
    n_spec = pl.BlockSpec((1, N_HEADS, 1, HEAD_DK), lambda i, *_: (i, 0, 0, 0))
    in_specs = [qk_spec, dv_spec, dv_spec,
                pl.BlockSpec((1, MAIN_DV), lambda i, *_: (0, 0)), st_spec]
    args = [qk, v, z, g_head.reshape(1, MAIN_DV), state]
    out_specs = [dv_spec, st_spec]
    out_shape = [jax.ShapeDtypeStruct((b, 1, MAIN_DV), BF16),
                 jax.ShapeDtypeStruct(state.shape, F32)]
    if use_rope:
        tab = pl.BlockSpec((1, HEAD_DK // 2), lambda i, *_: (0, 0))
        in_specs += [tab, tab]
        args += list(rope_tabs)
    if is_mlstm:
        in_specs += [dv_spec, n_spec]
        args += [og, n_state]
        out_specs.append(n_spec)
        out_shape.append(jax.ShapeDtypeStruct(n_state.shape, F32))
    return pl.pallas_call(
        functools.partial(_decode_body, use_rope=use_rope, is_mlstm=is_mlstm),
        grid_spec=pltpu.PrefetchScalarGridSpec(
            num_scalar_prefetch=3, grid=(b,),
            in_specs=in_specs, out_specs=out_specs),
        out_shape=out_shape,
        compiler_params=_cparams(("parallel",)),
        name="mlstm_decode" if is_mlstm else "ret_decode",
    )(a, w, e, *args)


def _gate_decode_body(gi_ref, gf_ref, bi_ref, bf_ref, m_ref, mo_ref, a_ref, w_ref, e_ref):
    ic = gi_ref[...] + bi_ref[...]
    lf = _log_sigmoid(gf_ref[...] + bf_ref[...])
    log_prev = lf + m_ref[...]
    m_t = jnp.maximum(log_prev, ic)
    mo_ref[...] = m_t
    a_ref[...] = jnp.exp(log_prev - m_t)
    w_ref[...] = jnp.exp(ic - m_t)
    e_ref[...] = jnp.exp(-m_t)


def _gate_decode(g_i, g_f, b_i, b_f, m):
    sds = jax.ShapeDtypeStruct(m.shape, F32)
    return pl.pallas_call(
        _gate_decode_body, out_shape=[sds, sds, sds, sds], name="mlstm_decode_gates",
    )(g_i, g_f, b_i.reshape(1, N_HEADS), b_f.reshape(1, N_HEADS), m)


def _lane_tiled(x):
    lead = x.shape[:-2]
    nh, d = x.shape[-2:]
    x = x.reshape(lead + (nh, d // 128, 128))
    x = jnp.swapaxes(x, -3, -2)
    return x.reshape(lead + (d // 128 * nh, 128))


def _memattn_decode_body(q_ref, k_ref, v_ref, z_ref, u_ref):
    rows = MEM_HEAD_DIM // 128 * N_MEM_HEADS
    for bi in range(MEMDEC_ROWS):
        q = q_ref[bi] * MEM_HEAD_DIM ** -0.5
        prod = k_ref[bi] * q[None]
        part = prod[:, :rows // 2, :] + prod[:, rows // 2:, :]
        part = part + pltpu.roll(part, N_MEM_HEADS, 1)
        s = jnp.sum(part, axis=-1, keepdims=True)
        p = jnp.exp(s - jnp.max(s, axis=0, keepdims=True))
        p = p / jnp.sum(p, axis=0, keepdims=True)
        o = jnp.concatenate([jnp.sum(p * v_ref[bi, :, :rows // 2, :], axis=0),
                             jnp.sum(p * v_ref[bi, :, rows // 2:, :], axis=0)], axis=0)
        z = z_ref[bi]
        u_ref[bi] = (o * (z * _sigmoid(z))).astype(u_ref.dtype)


def _memattn_decode(qm, cache_k, cache_v, z, layer):
    b = qm.shape[0]
    nb = MEMDEC_ROWS
    rows = qm.shape[1]
    row = pl.BlockSpec((nb, rows, 128), lambda i: (i, 0, 0))
    kv = pl.BlockSpec((None, nb, N_MEM, rows, 128), lambda i: (layer, i, 0, 0, 0))
    return pl.pallas_call(
        _memattn_decode_body,
        grid=(b // nb,),
        in_specs=[row, kv, kv, row],
        out_specs=row,
        out_shape=jax.ShapeDtypeStruct((b, rows, 128), BF16),
        compiler_params=_cparams(("parallel",)),
        name=f"memattn_decode_{layer}",
    )(qm, cache_k, cache_v, z)


def _from_lane_tiled(x, nh):
    lead = x.shape[:-2]
    nt = x.shape[-2] // nh
    x = x.reshape(lead + (nt, nh, 128))
    x = jnp.swapaxes(x, -3, -2)
    return x.reshape(lead + (nh, nt * 128))


def _rope_tables(pos):
    d = HEAD_DK
    inv = 1.0 / (ROPE_THETA ** (jnp.arange(0, d, 2, dtype=F32) / d))
    ang = pos[:, None] * inv[None, :]
    return jnp.cos(ang), jnp.sin(ang)


_REGIONS = (("qk", COL_QK, COL_V, F32), ("v", COL_V, COL_QM, BF16), ("qm", COL_QM, COL_Z, F32),
            ("z", COL_Z, COL_OG, F32), ("og", COL_OG, COL_GATES, F32))


def _in_proj(h, w, tag, with_mlstm):
    w_is_nk = with_mlstm
    wk = jnp.swapaxes(w, 1, 2) if w_is_nk else w
    out = {}
    for name, c0, c1, dt in _REGIONS:
        if name == "og" and not with_mlstm:
            continue
        out[name] = _matmul_w32(h, wk, 0, c0, c1 - c0, dt, f"inproj_{name}_{tag}", w_is_nk)
    if with_mlstm:
        wg = jnp.pad(w[0, :, COL_GATES:], ((0, 0), (0, 128 - 2 * N_HEADS))).astype(BF16)
        out["gates"] = _matmul(h, wg, F32, f"inproj_gates_{tag}")
    return out


def kernel(x_prompt, x_sample, mem_prompt, state_ret, state_mlstm_C, state_mlstm_n, state_mlstm_m,
           cache_mem_k, cache_mem_v, g_pre, g_post, w_in_ret, w_in_mlstm, b_gate_mlstm, g_head,
           w_out, g_mem, w_mem_kv):
    bp, lp, _ = x_prompt.shape
    bs = x_sample.shape[0]
    mp = bp * lp

    mem2d = mem_prompt.reshape(bp * N_MEM, D_MODEL)
    mem_kv = []
    for l in range(DEPTH):
        hm = _rmsnorm(mem2d, g_mem[l], f"mem_norm_{l}")
        mem_kv.append(_matmul_w32(hm, w_mem_kv, l, 0, 2 * MEM_DIM, F32, f"mem_kv_{l}"))

    cos_p, sin_p = _rope_tables(jnp.arange(lp, dtype=F32))
    cos_s, sin_s = _rope_tables(PAST_LEN + jnp.arange(1, dtype=F32))
    bias = b_gate_mlstm[0]
    xp = x_prompt.reshape(mp, D_MODEL)
    xs = x_sample.reshape(bs, D_MODEL)
    srow = lambda a: a[mp:].reshape(bs, 1, a.shape[-1])
    smem = lambda a: _lane_tiled(a[mp:].reshape(bs, N_MEM_HEADS, MEM_HEAD_DIM))
    cache_k = _lane_tiled(cache_mem_k)
    cache_v = _lane_tiled(cache_mem_v)
    unmem = lambda u: _from_lane_tiled(u, N_MEM_HEADS).reshape(bs, MEM_DIM)

    h = _rmsnorm_stacked(xp, xs, g_pre[0])
    pr = _in_proj(h, w_in_ret, "0", False)
    u_main_p, ret_prompt, w_o = _ret_prompt(pr["qk"], pr["v"], pr["z"], cos_p, sin_p, g_head[0],
                                            w_out, 0, bp, lp)
    u_mem_p = _memattn_prompt(pr["qm"], mem_kv[0], pr["z"], bp, lp)
    out_p = _outproj(u_main_p, u_mem_p, w_o, "outproj_p0")

    gamma = 1.0 - 2.0 ** (-5.0 - jnp.arange(N_HEADS, dtype=F32))
    a_ret = jnp.exp(jnp.log(gamma))
    a_ret = jnp.broadcast_to(a_ret[None, :], (bs, N_HEADS)).reshape(-1)
    ones = jnp.ones((bs * N_HEADS,), F32)
    u_main_s, ret_sample = _decode(a_ret, ones, ones, srow(pr["qk"]), srow(pr["v"]), srow(pr["z"]),
                                   g_head[0], state_ret[0], rope_tabs=(cos_s, sin_s))
    u_mem_s = _memattn_decode(smem(pr["qm"]), cache_k, cache_v, smem(pr["z"][:, MAIN_DV:]), 0)
    out_s = _outproj(u_main_s.reshape(bs, MAIN_DV), unmem(u_mem_s), w_o, "outproj_s0")
    x1, h = _post_mid(out_p, out_s, xp, xs, g_post[0], g_pre[1])

    pr = _in_proj(h, w_in_mlstm, "1", True)
    gates = pr["gates"][:, :2 * N_HEADS]
    gates_t = gates[:mp].reshape(bp, lp, 2 * N_HEADS).transpose(0, 2, 1)[:, :, None, :]
    u_main_p, c_prompt, n_prompt, m_prompt, w_o = _mlstm_prompt(
        pr["qk"], pr["v"], pr["og"], pr["z"], gates_t, bias, g_head[1], w_out, 1, bp, lp)
    u_mem_p = _memattn_prompt(pr["qm"], mem_kv[1], pr["z"], bp, lp)
    out_p = _outproj(u_main_p, u_mem_p, w_o, "outproj_p1")
    y_prompt = _post_last(out_p, x1, 0, g_post[1], "post_p1")

    g_i = gates[mp:, :N_HEADS]
    g_f = gates[mp:, N_HEADS:]
    m_new, a_m, w_m, e_m = _gate_decode(g_i, g_f, bias[:N_HEADS], bias[N_HEADS:], state_mlstm_m[0])
    u_main_s, c_sample, n_sample = _decode(
        a_m.reshape(-1), w_m.reshape(-1), e_m.reshape(-1), srow(pr["qk"]), srow(pr["v"]),
        srow(pr["z"]), g_head[1], state_mlstm_C[0], og=srow(pr["og"]),
        n_state=state_mlstm_n[0].reshape(bs, N_HEADS, 1, HEAD_DK))
    u_mem_s = _memattn_decode(smem(pr["qm"]), cache_k, cache_v, smem(pr["z"][:, MAIN_DV:]), 1)
    out_s = _outproj(u_main_s.reshape(bs, MAIN_DV), unmem(u_mem_s), w_o, "outproj_s1")
    y_sample = _post_last(out_s, x1, mp, g_post[1], "post_s1")

    shape_kv = (DEPTH, bp, N_MEM, N_MEM_HEADS, MEM_HEAD_DIM)
    return (
        y_prompt.reshape(bp, lp, D_MODEL),
        y_sample.reshape(bs, 1, D_MODEL),
        ret_prompt[None],
        c_prompt[None],
        n_prompt.reshape(1, bp, N_HEADS, HEAD_DK),
        m_prompt[:, :, 0, 0][None],
        jnp.stack([kv[:, :MEM_DIM] for kv in mem_kv]).reshape(shape_kv),
        jnp.stack([kv[:, MEM_DIM:] for kv in mem_kv]).reshape(shape_kv),
        ret_sample[None],
        c_sample[None],
        n_sample.reshape(1, bs, N_HEADS, HEAD_DK),
        m_new[None],
    )
```
